```python
import jax, jax.numpy as jnp
from jax import lax
import numpy as np

D_MODEL = 1024
BATCH = 32
SEQ = 2048
DEPTH = 4

N_A_LAYERS = DEPTH // 2
N_B_LAYERS = DEPTH - N_A_LAYERS

M_EXPAND = 2
M_D_INNER = M_EXPAND * D_MODEL
M_HEADDIM = 64
M_HEADS = M_D_INNER // M_HEADDIM
M_GROUPS = 4
M_D_STATE = 128
M_D_CONV = 4
M_CHUNK = 128
M_CONV_DIM = M_D_INNER + 2 * M_GROUPS * M_D_STATE
M_IN_DIM = 2 * M_D_INNER + 2 * M_GROUPS * M_D_STATE + M_HEADS

SB_HEAD_DIM = 64
SB_HEADS = D_MODEL // SB_HEAD_DIM
SB_WIDTH = SB_HEADS * SB_HEAD_DIM
SB_BLOCK = 128

EPS = 1e-6

kernel_name = 'yoco_mamba2_stickbreaking_adaln'


def _rmsnorm(x, g):
    xf = x.astype(jnp.float32)
    xf = xf * lax.rsqrt(jnp.mean(xf * xf, axis=-1, keepdims=True) + EPS)
    return (xf * g.astype(jnp.float32)).astype(x.dtype)


def _modulate(x, g, shift, scale):
    return _rmsnorm(x, g) * (1.0 + scale[:, None, :]) + shift[:, None, :]


def _causal_dwconv(u, w, b):
    k = w.shape[0]
    out = lax.conv_general_dilated(
        u, w[:, None, :].astype(u.dtype), window_strides=(1,), padding=[(k - 1, 0)],
        dimension_numbers=('NWC', 'WIO', 'NWC'), feature_group_count=u.shape[-1])
    return out + b


def _ssd_chunked(xs, dt, a_neg, bm, cm):
    b, s, h, p = xs.shape
    g, n = bm.shape[2], bm.shape[3]
    hg = h // g
    nc = s // M_CHUNK
    xdt = (xs.astype(jnp.float32) * dt[..., None]).reshape(b, s, g, hg, p)
    a = (dt * a_neg).reshape(b, s, g, hg)

    def chunks(t):
        return jnp.moveaxis(t.reshape((b, nc, M_CHUNK) + t.shape[2:]), 1, 0)

    causal = jnp.tril(jnp.ones((M_CHUNK, M_CHUNK), dtype=bool))[None, :, :, None, None]

    def step(state, inp):
        xc, ac, bc, cc = inp
        acum = jnp.cumsum(ac, axis=1)
        seg = jnp.where(causal, acum[:, :, None] - acum[:, None, :], -jnp.inf)
        scores = jnp.einsum('btgn,bsgn->btsg', cc, bc)[..., None] * jnp.exp(seg)
        y_diag = jnp.einsum('btsgh,bsghp->btghp', scores, xc)
        y_off = jnp.einsum('btgn,bghpn->btghp', cc, state) * jnp.exp(acum)[..., None]
        w_end = jnp.exp(acum[:, -1:] - acum)
        new_state = (state * jnp.exp(acum[:, -1])[..., None, None]
                     + jnp.einsum('bsgn,bsghp->bghpn', bc, xc * w_end[..., None]))
        return new_state, y_diag + y_off

    state0 = jnp.zeros((b, g, hg, p, n), jnp.float32)
    _, ys = lax.scan(step, state0, (chunks(xdt), chunks(a),
                                    chunks(bm.astype(jnp.float32)), chunks(cm.astype(jnp.float32))))
    return jnp.moveaxis(ys, 0, 1).reshape(b, s, h, p)


def _mamba2_mixer(h, in_w, conv_w, conv_b, dt_bias, a_log, d_skip, norm_g, out_w):
    b, s, _ = h.shape
    zxbcdt = h @ in_w
    z = zxbcdt[..., :M_D_INNER]
    xbc = zxbcdt[..., M_D_INNER:M_D_INNER + M_CONV_DIM]
    dt = zxbcdt[..., M_D_INNER + M_CONV_DIM:]
    xbc = jax.nn.silu(_causal_dwconv(xbc, conv_w, conv_b))
    gn = M_GROUPS * M_D_STATE
    xs = xbc[..., :M_D_INNER].reshape(b, s, M_HEADS, M_HEADDIM)
    bm = xbc[..., M_D_INNER:M_D_INNER + gn].reshape(b, s, M_GROUPS, M_D_STATE)
    cm = xbc[..., M_D_INNER + gn:].reshape(b, s, M_GROUPS, M_D_STATE)
    dt = jax.nn.softplus(dt.astype(jnp.float32) + dt_bias.astype(jnp.float32))
    a_neg = -jnp.exp(a_log.astype(jnp.float32))
    y = _ssd_chunked(xs, dt, a_neg, bm, cm)
    y = y + d_skip.astype(jnp.float32)[:, None] * xs.astype(jnp.float32)
    y = y.reshape(b, s, M_D_INNER) * jax.nn.silu(z.astype(jnp.float32))
    yg = y.reshape(b, s, M_GROUPS, M_D_INNER // M_GROUPS)
    yg = yg * lax.rsqrt(jnp.mean(yg * yg, axis=-1, keepdims=True) + EPS)
    y = yg.reshape(b, s, M_D_INNER) * norm_g.astype(jnp.float32)
    return y.astype(h.dtype) @ out_w


def _stick_breaking(q, k, v):
    s_len = q.shape[1]
    scale = SB_HEAD_DIM ** -0.5
    outs = []
    for i in range(s_len // SB_BLOCK):
        lo, hi = i * SB_BLOCK, (i + 1) * SB_BLOCK
        z = jnp.einsum('bthd,bshd->bhts', q[:, lo:hi], k[:, :hi]).astype(jnp.float32) * scale
        mask = ((lo + jnp.arange(SB_BLOCK))[:, None] > jnp.arange(hi)[None, :])[None, None]
        log_keep = jnp.where(mask, jax.nn.log_sigmoid(-z), 0.0)
        after = lax.cumsum(log_keep, axis=3, reverse=True) - log_keep
        log_a = jnp.where(mask, jax.nn.log_sigmoid(z) + after, -jnp.inf)
        a = jnp.exp(log_a).astype(v.dtype)
        outs.append(jnp.einsum('bhts,bshd->bthd', a, v[:, :hi]))
    return jnp.concatenate(outs, axis=1)


def _sb_mixer(h, k, v, in_w, out_w):
    b, s, _ = h.shape
    qz = h @ in_w
    q = qz[..., :SB_WIDTH].reshape(b, s, SB_HEADS, SB_HEAD_DIM)
    z = qz[..., SB_WIDTH:]
    o = _stick_breaking(q, k, v).reshape(b, s, SB_WIDTH)
    return (o * jax.nn.silu(z)) @ out_w


def setup_inputs(seed: int = 0) -> dict:
    key = jax.random.key(seed)
    ks = jax.random.split(key, 24)
    f32 = jnp.float32

    def nrm(k, shape, fan_in, mult=1.0):
        return jax.random.normal(k, shape, f32) * (mult * fan_in ** -0.5)

    def gain(k, shape):
        return 1.0 + 0.05 * jax.random.normal(k, shape, f32)

    def small(k, shape):
        return 0.01 * jax.random.normal(k, shape, f32)

    x = jax.random.normal(ks[0], (BATCH, SEQ, D_MODEL), f32)
    c = jax.random.normal(ks[1], (BATCH, D_MODEL), f32)
    ada_w = nrm(ks[2], (DEPTH, D_MODEL, 3 * D_MODEL), D_MODEL, 0.5)
    ada_b = small(ks[3], (DEPTH, 3 * D_MODEL))
    norm_g = gain(ks[4], (DEPTH, D_MODEL))
    m_in_w = nrm(ks[5], (N_A_LAYERS, D_MODEL, M_IN_DIM), D_MODEL)
    m_conv_w = nrm(ks[6], (N_A_LAYERS, M_D_CONV, M_CONV_DIM), M_D_CONV)
    m_conv_b = small(ks[7], (N_A_LAYERS, M_CONV_DIM))
    dt0 = jnp.exp(jax.random.uniform(ks[8], (N_A_LAYERS, M_HEADS), f32,
                                     np.log(1e-3).astype(np.float32), np.log(1e-1).astype(np.float32)))
    m_dt_bias = dt0 + jnp.log(-jnp.expm1(-dt0))
    m_a_log = jnp.log(jax.random.uniform(ks[9], (N_A_LAYERS, M_HEADS), f32, 1.0, 16.0))
    m_d = 1.0 + 0.1 * jax.random.normal(ks[10], (N_A_LAYERS, M_HEADS), f32)
    m_norm_g = gain(ks[11], (N_A_LAYERS, M_D_INNER))
    m_out_w = nrm(ks[12], (N_A_LAYERS, M_D_INNER, D_MODEL), M_D_INNER)
    kv_ada_w = nrm(ks[13], (D_MODEL, 2 * D_MODEL), D_MODEL, 0.5)
    kv_ada_b = small(ks[14], (2 * D_MODEL,))
    kv_norm_g = gain(ks[15], (D_MODEL,))
    kv_w = nrm(ks[16], (D_MODEL, 2 * SB_WIDTH), D_MODEL)
    sb_in_w = nrm(ks[17], (N_B_LAYERS, D_MODEL, 2 * SB_WIDTH), D_MODEL)
    sb_out_w = nrm(ks[18], (N_B_LAYERS, SB_WIDTH, D_MODEL), SB_WIDTH)
    final_g = gain(ks[19], (D_MODEL,))
    return {'x': x, 'c': c, 'ada_w': ada_w, 'ada_b': ada_b, 'norm_g': norm_g,
            'm_in_w': m_in_w, 'm_conv_w': m_conv_w, 'm_conv_b': m_conv_b,
            'm_dt_bias': m_dt_bias, 'm_a_log': m_a_log, 'm_d': m_d,
            'm_norm_g': m_norm_g, 'm_out_w': m_out_w,
            'kv_ada_w': kv_ada_w, 'kv_ada_b': kv_ada_b, 'kv_norm_g': kv_norm_g, 'kv_w': kv_w,
            'sb_in_w': sb_in_w, 'sb_out_w': sb_out_w, 'final_g': final_g}


def reference(x, c, ada_w, ada_b, norm_g, m_in_w, m_conv_w, m_conv_b, m_dt_bias, m_a_log,
              m_d, m_norm_g, m_out_w, kv_ada_w, kv_ada_b, kv_norm_g, kv_w,
              sb_in_w, sb_out_w, final_g):
    b, s, _ = x.shape
    c_act = jax.nn.silu(c)

    for i in range(N_A_LAYERS):
        shift, scale, gate = jnp.split(c_act @ ada_w[i] + ada_b[i], 3, axis=-1)
        h = _modulate(x, norm_g[i], shift, scale)
        y = _mamba2_mixer(h, m_in_w[i], m_conv_w[i], m_conv_b[i], m_dt_bias[i], m_a_log[i],
                          m_d[i], m_norm_g[i], m_out_w[i])
        x = x + gate[:, None, :] * y

    kv_shift, kv_scale = jnp.split(c_act @ kv_ada_w + kv_ada_b, 2, axis=-1)
    hk = _modulate(x, kv_norm_g, kv_shift, kv_scale)
    kv = hk @ kv_w
    k = kv[..., :SB_WIDTH].reshape(b, s, SB_HEADS, SB_HEAD_DIM)
    v = kv[..., SB_WIDTH:].reshape(b, s, SB_HEADS, SB_HEAD_DIM)

    for j in range(N_B_LAYERS):
        i = N_A_LAYERS + j
        shift, scale, gate = jnp.split(c_act @ ada_w[i] + ada_b[i], 3, axis=-1)
        h = _modulate(x, norm_g[i], shift, scale)
        y = _sb_mixer(h, k, v, sb_in_w[j], sb_out_w[j])
        x = x + gate[:, None, :] * y

    return _rmsnorm(x, final_g)
```

```python
import functools

import jax
import jax.numpy as jnp
from jax import lax
from jax.experimental import pallas as pl
from jax.experimental.pallas import tpu as pltpu

D_MODEL = 1024
DEPTH = 4
N_A_LAYERS = DEPTH // 2
N_B_LAYERS = DEPTH - N_A_LAYERS

M_D_INNER = 2048
M_HEADDIM = 64
M_HEADS = M_D_INNER // M_HEADDIM
M_GROUPS = 4
M_HEADS_PER_GROUP = M_HEADS // M_GROUPS
M_D_STATE = 128
M_D_CONV = 4
M_CHUNK = 128
M_GN = M_GROUPS * M_D_STATE
M_CONV_DIM = M_D_INNER + 2 * M_GN
M_GROUP_ROWS = M_D_INNER // M_GROUPS

SB_HEAD_DIM = 64
SB_HEADS = D_MODEL // SB_HEAD_DIM
SB_WIDTH = SB_HEADS * SB_HEAD_DIM
SB_BLOCK = 128
SB_PAIRS = SB_HEADS // 2

EPS = 1e-6
LANES = 128
F32_EXP_ZERO = 104.0

MAMBA_TILE = 256
SB_TILE = 256
KV_TILE = 512
NORM_TILE = 512
VMEM_LIMIT = 56 * 1024 * 1024

f32 = jnp.float32
bf16 = jnp.bfloat16


def _sigmoid(v):
    return 1.0 / (1.0 + jnp.exp(-v))


def _softplus(v):
    return jnp.maximum(v, 0.0) + jnp.log(1.0 + jnp.exp(-jnp.abs(v)))


def _split3(v):
    hi = v.astype(bf16)
    r1 = v - hi.astype(f32)
    mid = r1.astype(bf16)
    lo = (r1 - mid.astype(f32)).astype(bf16)
    return hi, mid, lo


def _const_spec(shape):
    nd = len(shape)
    return pl.BlockSpec(shape, lambda *_: (0,) * nd, pipeline_mode=pl.Buffered(1))


def _modulated_norm(x, ma_ref, ms_ref, h_ref):
    ts = x.shape[1]
    r = lax.rsqrt(jnp.mean(x * x, axis=0, keepdims=True) + EPS)
    for cb in range(ts // LANES):
        sl = slice(cb * LANES, (cb + 1) * LANES)
        h_ref[:, sl] = ((x[:, sl] * r[:, sl]) * ma_ref[0] + ms_ref[0]).astype(bf16)


def _ada_kernel(c_ref, w_ref, b_ref, o_ref):
    c = c_ref[...]
    ca = c * _sigmoid(c)
    o_ref[0] = jnp.dot(ca, w_ref[0], preferred_element_type=f32,
                       precision=lax.Precision.HIGHEST) + b_ref[0]


def _ada(c, w, b):
    nl, d, n = w.shape
    bsz = c.shape[0]
    tn = 1024
    return pl.pallas_call(
        _ada_kernel,
        grid=(nl, n // tn),
        in_specs=[pl.BlockSpec((bsz, d), lambda i, k: (0, 0)),
                  pl.BlockSpec((1, d, tn), lambda i, k: (i, 0, k)),
                  pl.BlockSpec((1, 1, tn), lambda i, k: (i, 0, k))],
        out_specs=pl.BlockSpec((1, bsz, tn), lambda i, k: (i, 0, k)),
        out_shape=jax.ShapeDtypeStruct((nl, bsz, n), f32),
        name="ada_mod",
    )(c, w, b.reshape(nl, 1, n))


def _mamba_kernel(x_ref, ma_ref, ms_ref, mg_ref, wz_ref, wx_ref, wdt_ref, cw_ref, cb_ref,
                  dtb_ref, alog_ref, dsk_ref, ng_ref, wo_ref,
                  o_ref,
                  h_s, u_s, xc_s, z_s, y_s, yn_s, xw_s, st_s, acum_s, dt_s, wd_s, etot_s, eac_s,
                  *, ts):
    j = pl.program_id(1)
    ncb = ts // LANES

    @pl.when(j == 0)
    def _():
        st_s[...] = jnp.zeros_like(st_s)
        u_s[:, :LANES] = jnp.zeros((M_CONV_DIM, LANES), f32)

    x = x_ref[0]
    _modulated_norm(x, ma_ref, ms_ref, h_s)
    h = h_s[...]

    z_s[...] = jnp.dot(wz_ref[...], h, preferred_element_type=f32)
    u_s[:, LANES:] = jnp.dot(wx_ref[...], h, preferred_element_type=f32)
    dtraw = jnp.dot(wdt_ref[...], h, preferred_element_type=f32)

    rc_rows = 256

    def conv_body(rc, carry):
        r0 = pl.multiple_of(rc * rc_rows, rc_rows)
        ue = u_s[pl.ds(r0, rc_rows), :]
        taps = [pltpu.roll(ue, M_D_CONV - 1 - k, axis=1)[:, LANES:] for k in range(M_D_CONV - 1)]
        taps.append(ue[:, LANES:])
        for cb in range(ncb):
            sl = slice(cb * LANES, (cb + 1) * LANES)
            acc = cb_ref[pl.ds(r0, rc_rows), :]
            for k in range(M_D_CONV):
                acc = acc + cw_ref[k, pl.ds(r0, rc_rows), :] * taps[k][:, sl]
            xc_s[pl.ds(r0, rc_rows), sl] = acc * _sigmoid(acc)
        return carry

    lax.fori_loop(0, M_CONV_DIM // rc_rows, conv_body, 0)
    u_s[:, :LANES] = u_s[:, ts:ts + LANES]

    a_neg = -jnp.exp(alog_ref[...])
    ri = lax.broadcasted_iota(jnp.int32, (M_CHUNK, 2 * M_CHUNK), 0)
    ci = lax.broadcasted_iota(jnp.int32, (M_CHUNK, 2 * M_CHUNK), 1)
    uo = jnp.where((ri <= ci) | (ci >= M_CHUNK), 1.0, 0.0).astype(bf16)
    uo3 = jnp.concatenate([uo, uo, uo], axis=0)
    for c in range(ncb):
        cols = slice(c * LANES, (c + 1) * LANES)
        dt = _softplus(dtraw[:, cols] + dtb_ref[...])
        a = dt * a_neg
        at = jnp.dot(jnp.concatenate(_split3(a), axis=1), uo3, preferred_element_type=f32)
        acum = at[:, :M_CHUNK]
        tot = at[:, M_CHUNK:]
        acum_s[c] = acum
        dt_s[c] = dt
        wd_s[c] = jnp.exp(tot - acum) * dt
        etot_s[c] = jnp.exp(tot)
        eac_s[c] = jnp.exp(acum)

    si = lax.broadcasted_iota(jnp.int32, (M_CHUNK, M_CHUNK), 0)
    ti = lax.broadcasted_iota(jnp.int32, (M_CHUNK, M_CHUNK), 1)
    causal = si <= ti

    for c in range(ncb):
        cols = slice(c * LANES, (c + 1) * LANES)

        def group_body(g, carry, c=c, cols=cols):
            b_t = xc_s[pl.ds(pl.multiple_of(M_D_INNER + g * M_D_STATE, M_D_STATE), M_D_STATE), cols]
            c_t = xc_s[pl.ds(pl.multiple_of(M_D_INNER + M_GN + g * M_D_STATE, M_D_STATE), M_D_STATE), cols]
            b_n = b_t.T.astype(bf16)
            c_tb = c_t.astype(bf16)
            sc_t = jnp.dot(b_n, c_tb, preferred_element_type=f32)
            r0 = pl.multiple_of(g * M_GROUP_ROWS, M_GROUP_ROWS)
            st_g = st_s[pl.ds(r0, M_GROUP_ROWS), :]
            yoff = jnp.dot(st_g.astype(bf16), c_tb, preferred_element_type=f32)
            for hh in range(M_HEADS_PER_GROUP):
                hrow = g * M_HEADS_PER_GROUP + hh
                rows = pl.ds(pl.multiple_of(r0 + hh * M_HEADDIM, M_HEADDIM), M_HEADDIM)
                hsl = slice(hh * M_HEADDIM, (hh + 1) * M_HEADDIM)
                ac_row = acum_s[c, pl.ds(hrow, 1), :]
                rowb = jnp.broadcast_to(ac_row, (M_CHUNK, M_CHUNK))
                colb = rowb.T
                dec = jnp.exp(jnp.where(causal, rowb - colb, -1e30))
                mt = (sc_t * dec).astype(bf16)
                xs_h = xc_s[rows, cols]
                xdt = (xs_h * dt_s[c, pl.ds(hrow, 1), :]).astype(bf16)
                yd = jnp.dot(xdt, mt, preferred_element_type=f32)
                y_s[rows, cols] = (yd + yoff[hsl] * eac_s[c, pl.ds(hrow, 1), :]
                                   + dsk_ref[rows, :] * xs_h)
                xw_s[hsl, :] = (xs_h * wd_s[c, pl.ds(hrow, 1), :]).astype(bf16)
                st_s[rows, :] = st_g[hsl] * etot_s[c, pl.ds(hrow, 1), :]
            st_s[pl.ds(r0, M_GROUP_ROWS), :] += jnp.dot(xw_s[...], b_n, preferred_element_type=f32)
            return carry

        lax.fori_loop(0, M_GROUPS, group_body, 0)

    for g in range(M_GROUPS):
        rows = slice(g * M_GROUP_ROWS, (g + 1) * M_GROUP_ROWS)
        zg = z_s[rows, :]
        yv = y_s[rows, :] * (zg * _sigmoid(zg))
        r = lax.rsqrt(jnp.mean(yv * yv, axis=0, keepdims=True) + EPS)
        for cb in range(ncb):
            sl = slice(cb * LANES, (cb + 1) * LANES)
            yn_s[rows, sl] = ((yv[:, sl] * r[:, sl]) * ng_ref[rows, :]).astype(bf16)
    out = jnp.dot(wo_ref[...], yn_s[...], preferred_element_type=f32)
    for cb in range(ncb):
        sl = slice(cb * LANES, (cb + 1) * LANES)
        o_ref[0, :, sl] = x[:, sl] + mg_ref[0] * out[:, sl]


def _mamba_layer(xt, ma, ms, mg, wz_t, wx_t, wdt_t, cw, cb, dtb, alog, dsk, ng, wo_t):
    bsz, d, s = xt.shape
    ts = min(MAMBA_TILE, s)
    ncb = ts // LANES
    x_spec = pl.BlockSpec((1, d, ts), lambda b, j: (b, 0, j))
    mod_spec = pl.BlockSpec((1, d, LANES), lambda b, j: (b, 0, 0))
    small = pltpu.VMEM((ncb, M_HEADS, LANES), f32)
    return pl.pallas_call(
        functools.partial(_mamba_kernel, ts=ts),
        grid=(bsz, s // ts),
        in_specs=[x_spec, mod_spec, mod_spec, mod_spec,
                  _const_spec(wz_t.shape), _const_spec(wx_t.shape), _const_spec(wdt_t.shape),
                  _const_spec(cw.shape), _const_spec(cb.shape), _const_spec(dtb.shape),
                  _const_spec(alog.shape), _const_spec(dsk.shape), _const_spec(ng.shape),
                  _const_spec(wo_t.shape)],
        out_specs=x_spec,
        out_shape=jax.ShapeDtypeStruct(xt.shape, f32),
        scratch_shapes=[
            pltpu.VMEM((d, ts), bf16),
            pltpu.VMEM((M_CONV_DIM, LANES + ts), f32),
            pltpu.VMEM((M_CONV_DIM, ts), f32),
            pltpu.VMEM((M_D_INNER, ts), f32),
            pltpu.VMEM((M_D_INNER, ts), f32),
            pltpu.VMEM((M_D_INNER, ts), bf16),
            pltpu.VMEM((M_GROUP_ROWS, M_CHUNK), bf16),
            pltpu.VMEM((M_D_INNER, M_D_STATE), f32),
            small, small, small, small, small,
        ],
        compiler_params=pltpu.CompilerParams(
            dimension_semantics=("arbitrary", "arbitrary"), vmem_limit_bytes=VMEM_LIMIT),
        name="mamba_layer",
    )(xt, ma, ms, mg, wz_t, wx_t, wdt_t, cw, cb, dtb, alog, dsk, ng, wo_t)


def _kv_kernel(x_ref, ma_ref, ms_ref, w_ref, k_ref, vt_ref, h_s, *, ts):
    _modulated_norm(x_ref[0], ma_ref, ms_ref, h_s)
    kv = jnp.dot(w_ref[...], h_s[...], preferred_element_type=f32)
    k_ref[0] = kv[:SB_WIDTH].T.astype(bf16)
    for cb in range(ts // LANES):
        vt_ref[0, cb] = kv[SB_WIDTH:, cb * LANES:(cb + 1) * LANES].astype(bf16)


def _kv_proj(xt, ma, ms, wkv_t):
    bsz, d, s = xt.shape
    ts = min(KV_TILE, s)
    mod_spec = pl.BlockSpec((1, d, LANES), lambda b, j: (b, 0, 0))
    return pl.pallas_call(
        functools.partial(_kv_kernel, ts=ts),
        grid=(bsz, s // ts),
        in_specs=[pl.BlockSpec((1, d, ts), lambda b, j: (b, 0, j)), mod_spec, mod_spec,
                  _const_spec(wkv_t.shape)],
        out_specs=[pl.BlockSpec((1, ts, SB_WIDTH), lambda b, j: (b, j, 0)),
                   pl.BlockSpec((1, ts // LANES, SB_WIDTH, LANES), lambda b, j: (b, j, 0, 0))],
        out_shape=[jax.ShapeDtypeStruct((bsz, s, SB_WIDTH), bf16),
                   jax.ShapeDtypeStruct((bsz, s // LANES, SB_WIDTH, LANES), bf16)],
        scratch_shapes=[pltpu.VMEM((d, ts), bf16)],
        compiler_params=pltpu.CompilerParams(
            dimension_semantics=("arbitrary", "arbitrary"), vmem_limit_bytes=VMEM_LIMIT),
        name="kv_proj",
    )(xt, ma, ms, wkv_t)


def _sb_kernel(x_ref, ma_ref, ms_ref, mg_ref, win_ref, wout_ref, k_ref, vt_ref,
               o_ref,
               h_s, q_s, g_s, qp_s, acc_s, og_s,
               *, tq):
    j = pl.program_id(1)
    nsub = tq // SB_BLOCK
    x = x_ref[0]
    _modulated_norm(x, ma_ref, ms_ref, h_s)
    qz = jnp.dot(win_ref[...], h_s[...], preferred_element_type=f32)
    q_s[...] = (qz[:SB_WIDTH] * (SB_HEAD_DIM ** -0.5)).astype(bf16)
    gz = qz[SB_WIDTH:]
    g_s[...] = gz * _sigmoid(gz)

    row = lax.broadcasted_iota(jnp.int32, (SB_BLOCK, 2 * SB_BLOCK), 0)
    lane = lax.broadcasted_iota(jnp.int32, (SB_BLOCK, 2 * SB_BLOCK), 1)
    own_head = (row < SB_HEAD_DIM) == (lane < SB_BLOCK)
    strictly_before = row < (lane & (SB_BLOCK - 1))
    r2 = lax.broadcasted_iota(jnp.int32, (SB_BLOCK, 2 * SB_BLOCK), 0)
    c2 = lax.broadcasted_iota(jnp.int32, (SB_BLOCK, 2 * SB_BLOCK), 1) & (SB_BLOCK - 1)
    uu = jnp.where(c2 >= r2, 1.0, 0.0).astype(bf16)
    top = lax.broadcasted_iota(jnp.int32, (SB_BLOCK, SB_BLOCK), 0) < SB_HEAD_DIM

    for sub in range(nsub):
        qcols = slice(sub * SB_BLOCK, (sub + 1) * SB_BLOCK)
        qi = j * nsub + sub
        for p in range(SB_PAIRS):
            qp = q_s[p * SB_BLOCK:(p + 1) * SB_BLOCK, qcols]
            qp_s[p] = jnp.where(own_head, jnp.concatenate([qp, qp], axis=1), jnp.zeros_like(uu))

        def block_step(kb, cs, masked):
            k0 = pl.multiple_of(kb * SB_BLOCK, SB_BLOCK)
            new_cs = []
            for p in range(SB_PAIRS):
                prow = slice(p * SB_BLOCK, (p + 1) * SB_BLOCK)
                kblk = k_ref[0, pl.ds(k0, SB_BLOCK), prow]
                z = jnp.dot(kblk, qp_s[p], preferred_element_type=f32)
                sp = _softplus(z)
                if masked:
                    sp = jnp.where(strictly_before, sp, 0.0)
                hi = sp.astype(bf16)
                lo = (sp - hi.astype(f32)).astype(bf16)
                rin = jnp.dot(uu, jnp.concatenate([hi, lo], axis=0),
                              preferred_element_type=f32)
                csr = cs[p:p + 1, :]
                log_a = z - rin - csr
                if masked:
                    log_a = jnp.where(strictly_before, log_a, -1e30)
                a = jnp.exp(log_a).astype(bf16)
                contrib = jnp.dot(vt_ref[0, kb, prow, :], a, preferred_element_type=f32)
                if masked:
                    acc_s[p] = contrib
                else:
                    acc_s[p] += contrib
                new_cs.append(csr + rin[0:1, :])
            return jnp.concatenate(new_cs, axis=0)

        cs0 = block_step(qi, jnp.zeros((SB_PAIRS, 2 * SB_BLOCK), f32), True)

        def cond(carry):
            kb, cs = carry
            return jnp.logical_and(kb >= 0, jnp.min(cs) <= F32_EXP_ZERO)

        def body(carry):
            kb, cs = carry
            return kb - 1, block_step(kb, cs, False)

        lax.while_loop(cond, body, (qi - 1, cs0))

        for p in range(SB_PAIRS):
            prow = slice(p * SB_BLOCK, (p + 1) * SB_BLOCK)
            acc = acc_s[p]
            o_pair = jnp.where(top, acc[:, :SB_BLOCK], acc[:, SB_BLOCK:])
            og_s[prow, qcols] = (o_pair * g_s[prow, qcols]).astype(bf16)

    out = jnp.dot(wout_ref[...], og_s[...], preferred_element_type=f32)
    for cb in range(tq // LANES):
        sl = slice(cb * LANES, (cb + 1) * LANES)
        o_ref[0, :, sl] = x[:, sl] + mg_ref[0] * out[:, sl]


def _sb_layer(xt, ma, ms, mg, win_t, wout_t, k, vt):
    bsz, d, s = xt.shape
    tq = min(SB_TILE, s)
    x_spec = pl.BlockSpec((1, d, tq), lambda b, j: (b, 0, j))
    mod_spec = pl.BlockSpec((1, d, LANES), lambda b, j: (b, 0, 0))
    return pl.pallas_call(
        functools.partial(_sb_kernel, tq=tq),
        grid=(bsz, s // tq),
        in_specs=[x_spec, mod_spec, mod_spec, mod_spec,
                  _const_spec(win_t.shape), _const_spec(wout_t.shape),
                  pl.BlockSpec((1, s, SB_WIDTH), lambda b, j: (b, 0, 0)),
                  pl.BlockSpec((1, s // LANES, SB_WIDTH, LANES), lambda b, j: (b, 0, 0, 0))],
        out_specs=x_spec,
        out_shape=jax.ShapeDtypeStruct(xt.shape, f32),
        scratch_shapes=[
            pltpu.VMEM((d, tq), bf16),
            pltpu.VMEM((SB_WIDTH, tq), bf16),
            pltpu.VMEM((SB_WIDTH, tq), f32),
            pltpu.VMEM((SB_PAIRS, SB_BLOCK, 2 * SB_BLOCK), bf16),
            pltpu.VMEM((SB_PAIRS, SB_BLOCK, 2 * SB_BLOCK), f32),
            pltpu.VMEM((SB_WIDTH, tq), bf16),
        ],
        compiler_params=pltpu.CompilerParams(
            dimension_semantics=("arbitrary", "arbitrary"), vmem_limit_bytes=VMEM_LIMIT),
        name="sb_layer",
    )(xt, ma, ms, mg, win_t, wout_t, k, vt)


def _final_kernel(x_ref, g_ref, o_ref, *, ts):
    x = x_ref[0]
    r = lax.rsqrt(jnp.mean(x * x, axis=0, keepdims=True) + EPS)
    for cb in range(ts // LANES):
        sl = slice(cb * LANES, (cb + 1) * LANES)
        o_ref[0, sl, :] = ((x[:, sl] * r[:, sl]) * g_ref[...]).T


def _final_norm(xt, g_rep):
    bsz, d, s = xt.shape
    ts = min(NORM_TILE, s)
    return pl.pallas_call(
        functools.partial(_final_kernel, ts=ts),
        grid=(bsz, s // ts),
        in_specs=[pl.BlockSpec((1, d, ts), lambda b, j: (b, 0, j)), _const_spec(g_rep.shape)],
        out_specs=pl.BlockSpec((1, ts, d), lambda b, j: (b, j, 0)),
        out_shape=jax.ShapeDtypeStruct((bsz, s, d), f32),
        compiler_params=pltpu.CompilerParams(
            dimension_semantics=("arbitrary", "arbitrary"), vmem_limit_bytes=VMEM_LIMIT),
        name="final_norm",
    )(xt, g_rep)


def _rep(v):
    return jnp.broadcast_to(v[..., None], v.shape + (LANES,))


def _layer_mods(mod, g):
    shift, scale, gate = jnp.split(mod, 3, axis=-1)
    return _rep(g[None, :] * (1.0 + scale)), _rep(shift), _rep(gate)


def kernel(x, c, ada_w, ada_b, norm_g, m_in_w, m_conv_w, m_conv_b, m_dt_bias, m_a_log, m_d, m_norm_g, m_out_w, kv_ada_w, kv_ada_b, kv_norm_g, kv_w, sb_in_w, sb_out_w, final_g):
    mods = _ada(c, ada_w, ada_b)
    kv_mod = _ada(c, kv_ada_w[None], kv_ada_b[None])[0]

    xt = jnp.swapaxes(x, 1, 2)

    for i in range(N_A_LAYERS):
        ma, ms, mg = _layer_mods(mods[i], norm_g[i])
        w = m_in_w[i]
        wz_t = w[:, :M_D_INNER].T.astype(bf16)
        wx_t = w[:, M_D_INNER:M_D_INNER + M_CONV_DIM].T.astype(bf16)
        wdt_t = w[:, M_D_INNER + M_CONV_DIM:].T.astype(bf16)
        xt = _mamba_layer(
            xt, ma, ms, mg, wz_t, wx_t, wdt_t,
            _rep(m_conv_w[i]), _rep(m_conv_b[i]), _rep(m_dt_bias[i]), _rep(m_a_log[i]),
            _rep(jnp.repeat(m_d[i], M_HEADDIM)), _rep(m_norm_g[i]),
            m_out_w[i].T.astype(bf16))

    kv_shift, kv_scale = jnp.split(kv_mod, 2, axis=-1)
    k, vt = _kv_proj(xt, _rep(kv_norm_g[None, :] * (1.0 + kv_scale)), _rep(kv_shift),
                     kv_w.T.astype(bf16))

    for jb in range(N_B_LAYERS):
        ma, ms, mg = _layer_mods(mods[N_A_LAYERS + jb], norm_g[N_A_LAYERS + jb])
        xt = _sb_layer(xt, ma, ms, mg, sb_in_w[jb].T.astype(bf16), sb_out_w[jb].T.astype(bf16), k, vt)

    return _final_norm(xt, jnp.broadcast_to(final_g[:, None], (D_MODEL, LANES)))
```

```python
import functools

import jax
import jax.numpy as jnp
from jax import lax
from jax.experimental import pallas as pl
from jax.experimental.pallas import tpu as pltpu

D_MODEL = 1024
DEPTH = 4
N_A_LAYERS = DEPTH // 2
N_B_LAYERS = DEPTH - N_A_LAYERS

M_D_INNER = 2048
M_HEADDIM = 64
M_HEADS = M_D_INNER // M_HEADDIM
M_GROUPS = 4
M_HEADS_PER_GROUP = M_HEADS // M_GROUPS
M_D_STATE = 128
M_D_CONV = 4
M_CHUNK = 128
M_GN = M_GROUPS * M_D_STATE
M_CONV_DIM = M_D_INNER + 2 * M_GN
M_GROUP_ROWS = M_D_INNER // M_GROUPS
CONV_HALO = 8

SB_HEAD_DIM = 64
SB_HEADS = D_MODEL // SB_HEAD_DIM
SB_WIDTH = SB_HEADS * SB_HEAD_DIM
SB_BLOCK = 128
SB_PAIRS = SB_HEADS // 2

EPS = 1e-6
LANES = 128
F32_EXP_ZERO = 104.0

MAMBA_TILE = 256
SB_TILE = 256
KV_TILE = 512
NORM_TILE = 512
VMEM_LIMIT = 56 * 1024 * 1024

f32 = jnp.float32
bf16 = jnp.bfloat16


def _sigmoid(v):
    return 1.0 / (1.0 + jnp.exp(-v))


def _softplus(v):
    return jnp.maximum(v, 0.0) + jnp.log(1.0 + jnp.exp(-jnp.abs(v)))


def _split3(v):
    hi = v.astype(bf16)
    r1 = v - hi.astype(f32)
    mid = r1.astype(bf16)
    lo = (r1 - mid.astype(f32)).astype(bf16)
    return hi, mid, lo


def _const_spec(shape):
    nd = len(shape)
    return pl.BlockSpec(shape, lambda *_: (0,) * nd, pipeline_mode=pl.Buffered(1))


def _modulated_norm(x, ma_ref, ms_ref, h_ref):
    ts = x.shape[1]
    r = lax.rsqrt(jnp.mean(x * x, axis=0, keepdims=True) + EPS)
    for cb in range(ts // LANES):
        sl = slice(cb * LANES, (cb + 1) * LANES)
        h_ref[:, sl] = ((x[:, sl] * r[:, sl]) * ma_ref[0] + ms_ref[0]).astype(bf16)


def _ada_kernel(c_ref, w_ref, b_ref, o_ref):
    c = c_ref[...]
    ca = c * _sigmoid(c)
    o_ref[0] = jnp.dot(ca, w_ref[0], preferred_element_type=f32,
                       precision=lax.Precision.HIGHEST) + b_ref[0]


def _ada(c, w, b):
    nl, d, n = w.shape
    bsz = c.shape[0]
    tn = 1024
    return pl.pallas_call(
        _ada_kernel,
        grid=(nl, n // tn),
        in_specs=[pl.BlockSpec((bsz, d), lambda i, k: (0, 0)),
                  pl.BlockSpec((1, d, tn), lambda i, k: (i, 0, k)),
                  pl.BlockSpec((1, 1, tn), lambda i, k: (i, 0, k))],
        out_specs=pl.BlockSpec((1, bsz, tn), lambda i, k: (i, 0, k)),
        out_shape=jax.ShapeDtypeStruct((nl, bsz, n), f32),
        name="ada_mod",
    )(c, w, b.reshape(nl, 1, n))


def _mamba_kernel(x_ref, ma_ref, ms_ref, mg_ref, wz_ref, wx_ref, wdt_ref, cw_ref, cb_ref,
                  dtb_ref, alog_ref, dsk_ref, ng_ref, wo_ref,
                  o_ref,
                  h_s, u_s, xc_s, b_s, c_s, z_s, y_s, yn_s, mt_s, xdt_s, xw_s, st_s,
                  acum_s, dt_s, wd_s, etot_s, eac_s,
                  *, ts):
    j = pl.program_id(1)
    ncb = ts // LANES
    halo = CONV_HALO

    @pl.when(j == 0)
    def _():
        st_s[...] = jnp.zeros_like(st_s)
        u_s[:halo, :] = jnp.zeros((halo, M_CONV_DIM), f32)

    x = x_ref[0]
    _modulated_norm(x, ma_ref, ms_ref, h_s)
    h = h_s[...]

    z_s[...] = jnp.dot(wz_ref[...], h, preferred_element_type=f32)
    dtraw = jnp.dot(wdt_ref[...], h, preferred_element_type=f32)
    u_s[halo:, :] = lax.dot_general(h, wx_ref[...], (((0,), (0,)), ((), ())),
                                    preferred_element_type=f32)

    for blk in range(M_CONV_DIM // LANES):
        cs_ = slice(blk * LANES, (blk + 1) * LANES)
        acc = cb_ref[:, cs_]
        for k in range(M_D_CONV):
            off = halo - (M_D_CONV - 1) + k
            acc = acc + cw_ref[k:k + 1, cs_] * u_s[off:off + ts, cs_]
        act = acc * _sigmoid(acc)
        if blk < M_D_INNER // LANES:
            xc_s[cs_, :] = act.T
        elif blk < (M_D_INNER + M_GN) // LANES:
            b_s[blk - M_D_INNER // LANES] = act
        else:
            g = blk - (M_D_INNER + M_GN) // LANES
            c_s[g * M_D_STATE:(g + 1) * M_D_STATE, :] = act.T
    u_s[:halo, :] = u_s[ts:ts + halo, :]

    a_neg = -jnp.exp(alog_ref[...])
    ri = lax.broadcasted_iota(jnp.int32, (M_CHUNK, 2 * M_CHUNK), 0)
    ci = lax.broadcasted_iota(jnp.int32, (M_CHUNK, 2 * M_CHUNK), 1)
    uo = jnp.where((ri <= ci) | (ci >= M_CHUNK), 1.0, 0.0).astype(bf16)
    uo3 = jnp.concatenate([uo, uo, uo], axis=0)
    for c in range(ncb):
        cols = slice(c * LANES, (c + 1) * LANES)
        dt = _softplus(dtraw[:, cols] + dtb_ref[...])
        a = dt * a_neg
        at = jnp.dot(jnp.concatenate(_split3(a), axis=1), uo3, preferred_element_type=f32)
        acum = at[:, :M_CHUNK]
        tot = at[:, M_CHUNK:]
        acum_s[c] = acum
        dt_s[c] = dt
        wd_s[c] = jnp.exp(tot - acum) * dt
        etot_s[c] = jnp.exp(tot)
        eac_s[c] = jnp.exp(acum)

    si = lax.broadcasted_iota(jnp.int32, (M_CHUNK, M_CHUNK), 0)
    ti = lax.broadcasted_iota(jnp.int32, (M_CHUNK, M_CHUNK), 1)
    causal = si <= ti

    def head_rows(g, hh):
        r = g * M_GROUP_ROWS + hh * M_HEADDIM
        return slice(r, r + M_HEADDIM)

    def decay_stage(c, g, slot):
        cols = slice(c * LANES, (c + 1) * LANES)
        b_n = b_s[g, cols, :].astype(bf16)
        c_tb = c_s[g * M_D_STATE:(g + 1) * M_D_STATE, cols].astype(bf16)
        sc_t = jnp.dot(b_n, c_tb, preferred_element_type=f32)
        for hh in range(M_HEADS_PER_GROUP):
            hrow = g * M_HEADS_PER_GROUP + hh
            hsl = slice(hh * M_HEADDIM, (hh + 1) * M_HEADDIM)
            rowb = jnp.broadcast_to(acum_s[c, hrow:hrow + 1, :], (M_CHUNK, M_CHUNK))
            dec = jnp.exp(jnp.where(causal, rowb - rowb.T, -1e30))
            mt_s[slot, hh] = (sc_t * dec).astype(bf16)
            xs_h = xc_s[head_rows(g, hh), cols]
            xdt_s[slot, hsl, :] = (xs_h * dt_s[c, hrow:hrow + 1, :]).astype(bf16)
            xw_s[slot, hsl, :] = (xs_h * wd_s[c, hrow:hrow + 1, :]).astype(bf16)

    def matmul_stage(c, g, slot):
        cols = slice(c * LANES, (c + 1) * LANES)
        grows = slice(g * M_GROUP_ROWS, (g + 1) * M_GROUP_ROWS)
        b_n = b_s[g, cols, :].astype(bf16)
        c_tb = c_s[g * M_D_STATE:(g + 1) * M_D_STATE, cols].astype(bf16)
        st_g = st_s[grows, :]
        yoff = jnp.dot(st_g.astype(bf16), c_tb, preferred_element_type=f32)
        for hh in range(M_HEADS_PER_GROUP):
            hrow = g * M_HEADS_PER_GROUP + hh
            hsl = slice(hh * M_HEADDIM, (hh + 1) * M_HEADDIM)
            rows = head_rows(g, hh)
            yd = jnp.dot(xdt_s[slot, hsl, :], mt_s[slot, hh], preferred_element_type=f32)
            y_s[rows, cols] = (yd + yoff[hsl] * eac_s[c, hrow:hrow + 1, :]
                               + dsk_ref[rows, :] * xc_s[rows, cols])
            st_s[rows, :] = st_g[hsl] * etot_s[c, hrow:hrow + 1, :]
        st_s[grows, :] += jnp.dot(xw_s[slot], b_n, preferred_element_type=f32)

    steps = [(c, g) for c in range(ncb) for g in range(M_GROUPS)]
    decay_stage(*steps[0], 0)
    for i, (c, g) in enumerate(steps):
        if i + 1 < len(steps):
            decay_stage(*steps[i + 1], (i + 1) % 2)
        matmul_stage(c, g, i % 2)

    for g in range(M_GROUPS):
        rows = slice(g * M_GROUP_ROWS, (g + 1) * M_GROUP_ROWS)
        zg = z_s[rows, :]
        yv = y_s[rows, :] * (zg * _sigmoid(zg))
        r = lax.rsqrt(jnp.mean(yv * yv, axis=0, keepdims=True) + EPS)
        for cb in range(ncb):
            sl = slice(cb * LANES, (cb + 1) * LANES)
            yn_s[rows, sl] = ((yv[:, sl] * r[:, sl]) * ng_ref[rows, :]).astype(bf16)
    out = jnp.dot(wo_ref[...], yn_s[...], preferred_element_type=f32)
    for cb in range(ncb):
        sl = slice(cb * LANES, (cb + 1) * LANES)
        o_ref[0, :, sl] = x[:, sl] + mg_ref[0] * out[:, sl]


def _mamba_layer(xt, ma, ms, mg, wz_t, wx, wdt_t, cw, cb, dtb, alog, dsk, ng, wo_t):
    bsz, d, s = xt.shape
    ts = min(MAMBA_TILE, s)
    ncb = ts // LANES
    x_spec = pl.BlockSpec((1, d, ts), lambda b, j: (b, 0, j))
    mod_spec = pl.BlockSpec((1, d, LANES), lambda b, j: (b, 0, 0))
    small = pltpu.VMEM((ncb, M_HEADS, LANES), f32)
    return pl.pallas_call(
        functools.partial(_mamba_kernel, ts=ts),
        grid=(bsz, s // ts),
        in_specs=[x_spec, mod_spec, mod_spec, mod_spec,
                  _const_spec(wz_t.shape), _const_spec(wx.shape), _const_spec(wdt_t.shape),
                  _const_spec(cw.shape), _const_spec(cb.shape), _const_spec(dtb.shape),
                  _const_spec(alog.shape), _const_spec(dsk.shape), _const_spec(ng.shape),
                  _const_spec(wo_t.shape)],
        out_specs=x_spec,
        out_shape=jax.ShapeDtypeStruct(xt.shape, f32),
        scratch_shapes=[
            pltpu.VMEM((d, ts), bf16),
            pltpu.VMEM((CONV_HALO + ts, M_CONV_DIM), f32),
            pltpu.VMEM((M_D_INNER, ts), f32),
            pltpu.VMEM((M_GROUPS, ts, M_D_STATE), f32),
            pltpu.VMEM((M_GN, ts), f32),
            pltpu.VMEM((M_D_INNER, ts), f32),
            pltpu.VMEM((M_D_INNER, ts), f32),
            pltpu.VMEM((M_D_INNER, ts), bf16),
            pltpu.VMEM((2, M_HEADS_PER_GROUP, M_CHUNK, M_CHUNK), bf16),
            pltpu.VMEM((2, M_GROUP_ROWS, M_CHUNK), bf16),
            pltpu.VMEM((2, M_GROUP_ROWS, M_CHUNK), bf16),
            pltpu.VMEM((M_D_INNER, M_D_STATE), f32),
            small, small, small, small, small,
        ],
        compiler_params=pltpu.CompilerParams(
            dimension_semantics=("arbitrary", "arbitrary"), vmem_limit_bytes=VMEM_LIMIT),
        name="mamba_layer",
    )(xt, ma, ms, mg, wz_t, wx, wdt_t, cw, cb, dtb, alog, dsk, ng, wo_t)


def _kv_kernel(x_ref, ma_ref, ms_ref, w_ref, k_ref, vt_ref, h_s, *, ts):
    _modulated_norm(x_ref[0], ma_ref, ms_ref, h_s)
    kv = jnp.dot(w_ref[...], h_s[...], preferred_element_type=f32)
    k_ref[0] = kv[:SB_WIDTH].T.astype(bf16)
    for cb in range(ts // LANES):
        vt_ref[0, cb] = kv[SB_WIDTH:, cb * LANES:(cb + 1) * LANES].astype(bf16)


def _kv_proj(xt, ma, ms, wkv_t):
    bsz, d, s = xt.shape
    ts = min(KV_TILE, s)
    mod_spec = pl.BlockSpec((1, d, LANES), lambda b, j: (b, 0, 0))
    return pl.pallas_call(
        functools.partial(_kv_kernel, ts=ts),
        grid=(bsz, s // ts),
        in_specs=[pl.BlockSpec((1, d, ts), lambda b, j: (b, 0, j)), mod_spec, mod_spec,
                  _const_spec(wkv_t.shape)],
        out_specs=[pl.BlockSpec((1, ts, SB_WIDTH), lambda b, j: (b, j, 0)),
                   pl.BlockSpec((1, ts // LANES, SB_WIDTH, LANES), lambda b, j: (b, j, 0, 0))],
        out_shape=[jax.ShapeDtypeStruct((bsz, s, SB_WIDTH), bf16),
                   jax.ShapeDtypeStruct((bsz, s // LANES, SB_WIDTH, LANES), bf16)],
        scratch_shapes=[pltpu.VMEM((d, ts), bf16)],
        compiler_params=pltpu.CompilerParams(
            dimension_semantics=("arbitrary", "arbitrary"), vmem_limit_bytes=VMEM_LIMIT),
        name="kv_proj",
    )(xt, ma, ms, wkv_t)


def _sb_kernel(x_ref, ma_ref, ms_ref, mg_ref, win_ref, wout_ref, k_ref, vt_ref,
               o_ref,
               h_s, q_s, g_s, qp_s, acc_s, og_s, z_s, hl_s, rin_s, a_s,
               *, tq):
    j = pl.program_id(1)
    nsub = tq // SB_BLOCK
    x = x_ref[0]
    _modulated_norm(x, ma_ref, ms_ref, h_s)
    qz = jnp.dot(win_ref[...], h_s[...], preferred_element_type=f32)
    q_s[...] = (qz[:SB_WIDTH] * (SB_HEAD_DIM ** -0.5)).astype(bf16)
    gz = qz[SB_WIDTH:]
    g_s[...] = gz * _sigmoid(gz)

    row = lax.broadcasted_iota(jnp.int32, (SB_BLOCK, 2 * SB_BLOCK), 0)
    lane = lax.broadcasted_iota(jnp.int32, (SB_BLOCK, 2 * SB_BLOCK), 1)
    own_head = (row < SB_HEAD_DIM) == (lane < SB_BLOCK)
    strictly_before = row < (lane & (SB_BLOCK - 1))
    r2 = lax.broadcasted_iota(jnp.int32, (SB_BLOCK, 2 * SB_BLOCK), 0)
    c2 = lax.broadcasted_iota(jnp.int32, (SB_BLOCK, 2 * SB_BLOCK), 1) & (SB_BLOCK - 1)
    uu = jnp.where(c2 >= r2, 1.0, 0.0).astype(bf16)
    top = lax.broadcasted_iota(jnp.int32, (SB_BLOCK, SB_BLOCK), 0) < SB_HEAD_DIM

    for sub in range(nsub):
        qcols = slice(sub * SB_BLOCK, (sub + 1) * SB_BLOCK)
        qi = j * nsub + sub
        for p in range(SB_PAIRS):
            qp = q_s[p * SB_BLOCK:(p + 1) * SB_BLOCK, qcols]
            qp_s[p] = jnp.where(own_head, jnp.concatenate([qp, qp], axis=1), jnp.zeros_like(uu))

        def block_step(kb, cs, masked):
            k0 = pl.multiple_of(kb * SB_BLOCK, SB_BLOCK)
            prows = [slice(p * SB_BLOCK, (p + 1) * SB_BLOCK) for p in range(SB_PAIRS)]
            new_cs = [None] * SB_PAIRS

            def logits(ps):
                for p in ps:
                    kblk = k_ref[0, pl.ds(k0, SB_BLOCK), prows[p]]
                    z_s[p] = jnp.dot(kblk, qp_s[p], preferred_element_type=f32)

            def log_keep(ps):
                for p in ps:
                    sp = _softplus(z_s[p])
                    if masked:
                        sp = jnp.where(strictly_before, sp, 0.0)
                    hi = sp.astype(bf16)
                    hl_s[p, :SB_BLOCK] = hi
                    hl_s[p, SB_BLOCK:] = (sp - hi.astype(f32)).astype(bf16)

            def cumsum(ps):
                for p in ps:
                    rin_s[p] = jnp.dot(uu, hl_s[p], preferred_element_type=f32)

            def weights(ps):
                for p in ps:
                    rin = rin_s[p]
                    csr = cs[p:p + 1, :]
                    log_a = z_s[p] - rin - csr
                    if masked:
                        log_a = jnp.where(strictly_before, log_a, -1e30)
                    a_s[p] = jnp.exp(log_a).astype(bf16)
                    new_cs[p] = csr + rin[0:1, :]

            def values(ps):
                for p in ps:
                    contrib = jnp.dot(vt_ref[0, kb, prows[p], :], a_s[p], preferred_element_type=f32)
                    if masked:
                        acc_s[p] = contrib
                    else:
                        acc_s[p] += contrib

            g0 = range(0, SB_PAIRS // 2)
            g1 = range(SB_PAIRS // 2, SB_PAIRS)
            logits(g0)
            logits(g1)
            log_keep(g0)
            cumsum(g0)
            log_keep(g1)
            weights(g0)
            cumsum(g1)
            values(g0)
            weights(g1)
            values(g1)
            return jnp.concatenate(new_cs, axis=0)

        cs0 = block_step(qi, jnp.zeros((SB_PAIRS, 2 * SB_BLOCK), f32), True)

        def cond(carry):
            kb, cs = carry
            return jnp.logical_and(kb >= 0, jnp.min(cs) <= F32_EXP_ZERO)

        def body(carry):
            kb, cs = carry
            return kb - 1, block_step(kb, cs, False)

        lax.while_loop(cond, body, (qi - 1, cs0))

        for p in range(SB_PAIRS):
            prow = slice(p * SB_BLOCK, (p + 1) * SB_BLOCK)
            acc = acc_s[p]
            o_pair = jnp.where(top, acc[:, :SB_BLOCK], acc[:, SB_BLOCK:])
            og_s[prow, qcols] = (o_pair * g_s[prow, qcols]).astype(bf16)

    out = jnp.dot(wout_ref[...], og_s[...], preferred_element_type=f32)
    for cb in range(tq // LANES):
        sl = slice(cb * LANES, (cb + 1) * LANES)
        o_ref[0, :, sl] = x[:, sl] + mg_ref[0] * out[:, sl]


def _sb_layer(xt, ma, ms, mg, win_t, wout_t, k, vt):
    bsz, d, s = xt.shape
    tq = min(SB_TILE, s)
    x_spec = pl.BlockSpec((1, d, tq), lambda b, j: (b, 0, j))
    mod_spec = pl.BlockSpec((1, d, LANES), lambda b, j: (b, 0, 0))
    return pl.pallas_call(
        functools.partial(_sb_kernel, tq=tq),
        grid=(bsz, s // tq),
        in_specs=[x_spec, mod_spec, mod_spec, mod_spec,
                  _const_spec(win_t.shape), _const_spec(wout_t.shape),
                  pl.BlockSpec((1, s, SB_WIDTH), lambda b, j: (b, 0, 0)),
                  pl.BlockSpec((1, s // LANES, SB_WIDTH, LANES), lambda b, j: (b, 0, 0, 0))],
        out_specs=x_spec,
        out_shape=jax.ShapeDtypeStruct(xt.shape, f32),
        scratch_shapes=[
            pltpu.VMEM((d, tq), bf16),
            pltpu.VMEM((SB_WIDTH, tq), bf16),
            pltpu.VMEM((SB_WIDTH, tq), f32),
            pltpu.VMEM((SB_PAIRS, SB_BLOCK, 2 * SB_BLOCK), bf16),
            pltpu.VMEM((SB_PAIRS, SB_BLOCK, 2 * SB_BLOCK), f32),
            pltpu.VMEM((SB_WIDTH, tq), bf16),
            pltpu.VMEM((SB_PAIRS, SB_BLOCK, 2 * SB_BLOCK), f32),
            pltpu.VMEM((SB_PAIRS, 2 * SB_BLOCK, 2 * SB_BLOCK), bf16),
            pltpu.VMEM((SB_PAIRS, SB_BLOCK, 2 * SB_BLOCK), f32),
            pltpu.VMEM((SB_PAIRS, SB_BLOCK, 2 * SB_BLOCK), bf16),
        ],
        compiler_params=pltpu.CompilerParams(
            dimension_semantics=("arbitrary", "arbitrary"), vmem_limit_bytes=VMEM_LIMIT),
        name="sb_layer",
    )(xt, ma, ms, mg, win_t, wout_t, k, vt)


def _final_kernel(x_ref, g_ref, o_ref, *, ts):
    x = x_ref[0]
    r = lax.rsqrt(jnp.mean(x * x, axis=0, keepdims=True) + EPS)
    for cb in range(ts // LANES):
        sl = slice(cb * LANES, (cb + 1) * LANES)
        o_ref[0, sl, :] = ((x[:, sl] * r[:, sl]) * g_ref[...]).T


def _final_norm(xt, g_rep):
    bsz, d, s = xt.shape
    ts = min(NORM_TILE, s)
    return pl.pallas_call(
        functools.partial(_final_kernel, ts=ts),
        grid=(bsz, s // ts),
        in_specs=[pl.BlockSpec((1, d, ts), lambda b, j: (b, 0, j)), _const_spec(g_rep.shape)],
        out_specs=pl.BlockSpec((1, ts, d), lambda b, j: (b, j, 0)),
        out_shape=jax.ShapeDtypeStruct((bsz, s, d), f32),
        compiler_params=pltpu.CompilerParams(
            dimension_semantics=("arbitrary", "arbitrary"), vmem_limit_bytes=VMEM_LIMIT),
        name="final_norm",
    )(xt, g_rep)


def _rep(v):
    return jnp.broadcast_to(v[..., None], v.shape + (LANES,))


def _layer_mods(mod, g):
    shift, scale, gate = jnp.split(mod, 3, axis=-1)
    return _rep(g[None, :] * (1.0 + scale)), _rep(shift), _rep(gate)


def kernel(x, c, ada_w, ada_b, norm_g, m_in_w, m_conv_w, m_conv_b, m_dt_bias, m_a_log, m_d, m_norm_g, m_out_w, kv_ada_w, kv_ada_b, kv_norm_g, kv_w, sb_in_w, sb_out_w, final_g):
    mods = _ada(c, ada_w, ada_b)
    kv_mod = _ada(c, kv_ada_w[None], kv_ada_b[None])[0]

    xt = jnp.swapaxes(x, 1, 2)

    for i in range(N_A_LAYERS):
        ma, ms, mg = _layer_mods(mods[i], norm_g[i])
        w = m_in_w[i]
        wz_t = w[:, :M_D_INNER].T.astype(bf16)
        wx = w[:, M_D_INNER:M_D_INNER + M_CONV_DIM].astype(bf16)
        wdt_t = w[:, M_D_INNER + M_CONV_DIM:].T.astype(bf16)
        xt = _mamba_layer(
            xt, ma, ms, mg, wz_t, wx, wdt_t,
            m_conv_w[i], m_conv_b[i][None, :], _rep(m_dt_bias[i]), _rep(m_a_log[i]),
            _rep(jnp.repeat(m_d[i], M_HEADDIM)), _rep(m_norm_g[i]),
            m_out_w[i].T.astype(bf16))

    kv_shift, kv_scale = jnp.split(kv_mod, 2, axis=-1)
    k, vt = _kv_proj(xt, _rep(kv_norm_g[None, :] * (1.0 + kv_scale)), _rep(kv_shift),
                     kv_w.T.astype(bf16))

    for jb in range(N_B_LAYERS):
        ma, ms, mg = _layer_mods(mods[N_A_LAYERS + jb], norm_g[N_A_LAYERS + jb])
        xt = _sb_layer(xt, ma, ms, mg, sb_in_w[jb].T.astype(bf16), sb_out_w[jb].T.astype(bf16), k, vt)

    return _final_norm(xt, jnp.broadcast_to(final_g[:, None], (D_MODEL, LANES)))
```

```python
import functools

import jax
import jax.numpy as jnp
from jax import lax
from jax.experimental import pallas as pl
from jax.experimental.pallas import tpu as pltpu

D_MODEL = 1024
DEPTH = 4
N_A_LAYERS = DEPTH // 2
N_B_LAYERS = DEPTH - N_A_LAYERS

M_D_INNER = 2048
M_HEADDIM = 64
M_HEADS = M_D_INNER // M_HEADDIM
M_GROUPS = 4
M_HEADS_PER_GROUP = M_HEADS // M_GROUPS
M_D_STATE = 128
M_D_CONV = 4
M_CHUNK = 128
M_GN = M_GROUPS * M_D_STATE
M_CONV_DIM = M_D_INNER + 2 * M_GN
M_GROUP_ROWS = M_D_INNER // M_GROUPS
CONV_HALO = 8

SB_HEAD_DIM = 64
SB_HEADS = D_MODEL // SB_HEAD_DIM
SB_WIDTH = SB_HEADS * SB_HEAD_DIM
SB_BLOCK = 128
SB_PAIRS = SB_HEADS // 2

EPS = 1e-6
LANES = 128
F32_EXP_FLUSH = 87.3366
SOFTPLUS_LINEAR = 40.0

MAMBA_TILE = 256
SB_TILE = 256
KV_TILE = 512
VMEM_LIMIT = 56 * 1024 * 1024

f32 = jnp.float32
bf16 = jnp.bfloat16


def _sigmoid(v):
    return 1.0 / (1.0 + jnp.exp(-v))


def _softplus(v):
    return jnp.maximum(v, 0.0) + jnp.log(1.0 + jnp.exp(-jnp.abs(v)))


def _softplus_direct(v):
    return jnp.where(v > SOFTPLUS_LINEAR, v, jnp.log(1.0 + jnp.exp(v)))


def _split3(v):
    hi = v.astype(bf16)
    r1 = v - hi.astype(f32)
    mid = r1.astype(bf16)
    lo = (r1 - mid.astype(f32)).astype(bf16)
    return hi, mid, lo


def _const_spec(shape):
    nd = len(shape)
    return pl.BlockSpec(shape, lambda *_: (0,) * nd, pipeline_mode=pl.Buffered(1))


def _stage_mods(m_ref, rep_s):
    for i in range(rep_s.shape[0]):
        row = m_ref[0, :, i * D_MODEL:(i + 1) * D_MODEL]
        rep_s[i] = jnp.broadcast_to(row, (LANES, D_MODEL)).T


def _modulated_norm(x, rep_s, h_ref):
    ts = x.shape[1]
    r = lax.rsqrt(jnp.mean(x * x, axis=0, keepdims=True) + EPS)
    for cb in range(ts // LANES):
        sl = slice(cb * LANES, (cb + 1) * LANES)
        h_ref[:, sl] = ((x[:, sl] * r[:, sl]) * rep_s[0] + rep_s[1]).astype(bf16)


def _ada_kernel(c_ref, w_ref, b_ref, o_ref):
    c = c_ref[...]
    ca = c * _sigmoid(c)
    o_ref[0] = jnp.dot(ca, w_ref[0], preferred_element_type=f32,
                       precision=lax.Precision.HIGHEST) + b_ref[0]


def _ada(c, w, b):
    nl, d, n = w.shape
    bsz = c.shape[0]
    tn = 1024
    return pl.pallas_call(
        _ada_kernel,
        grid=(nl, n // tn),
        in_specs=[pl.BlockSpec((bsz, d), lambda i, k: (0, 0)),
                  pl.BlockSpec((1, d, tn), lambda i, k: (i, 0, k)),
                  pl.BlockSpec((1, 1, tn), lambda i, k: (i, 0, k))],
        out_specs=pl.BlockSpec((1, bsz, tn), lambda i, k: (i, 0, k)),
        out_shape=jax.ShapeDtypeStruct((nl, bsz, n), f32),
        name="ada_mod",
    )(c, w, b.reshape(nl, 1, n))


def _mamba_kernel(x_ref, m_ref, wz_ref, wx_ref, wdt_ref, cw_ref, cb_ref,
                  dtb_ref, alog_ref, dsk_ref, ng_ref, wo_ref,
                  o_ref,
                  rep_s, h_s, u_s, xc_s, b_s, c_s, z_s, y_s, yn_s, mt_s, xdt_s, xw_s, st_s,
                  acum_s, dt_s, wd_s, etot_s, eac_s,
                  *, ts, x_token_major):
    j = pl.program_id(1)
    ncb = ts // LANES
    halo = CONV_HALO

    @pl.when(j == 0)
    def _():
        _stage_mods(m_ref, rep_s)
        st_s[...] = jnp.zeros_like(st_s)
        u_s[:halo, :] = jnp.zeros((halo, M_CONV_DIM), f32)

    x = x_ref[0].T if x_token_major else x_ref[0]
    _modulated_norm(x, rep_s, h_s)
    h = h_s[...]

    z_s[...] = jnp.dot(wz_ref[...], h, preferred_element_type=f32)
    dtraw = jnp.dot(wdt_ref[...], h, preferred_element_type=f32)
    u_s[halo:, :] = lax.dot_general(h, wx_ref[...], (((0,), (0,)), ((), ())),
                                    preferred_element_type=f32)

    for blk in range(M_CONV_DIM // LANES):
        cs_ = slice(blk * LANES, (blk + 1) * LANES)
        acc = cb_ref[:, cs_]
        for k in range(M_D_CONV):
            off = halo - (M_D_CONV - 1) + k
            acc = acc + cw_ref[k:k + 1, cs_] * u_s[off:off + ts, cs_]
        act = acc * _sigmoid(acc)
        if blk < M_D_INNER // LANES:
            xc_s[cs_, :] = act.T
        elif blk < (M_D_INNER + M_GN) // LANES:
            b_s[blk - M_D_INNER // LANES] = act
        else:
            g = blk - (M_D_INNER + M_GN) // LANES
            c_s[g * M_D_STATE:(g + 1) * M_D_STATE, :] = act.T
    u_s[:halo, :] = u_s[ts:ts + halo, :]

    a_neg = -jnp.exp(alog_ref[...])
    ri = lax.broadcasted_iota(jnp.int32, (M_CHUNK, 2 * M_CHUNK), 0)
    ci = lax.broadcasted_iota(jnp.int32, (M_CHUNK, 2 * M_CHUNK), 1)
    uo = jnp.where((ri <= ci) | (ci >= M_CHUNK), 1.0, 0.0).astype(bf16)
    uo3 = jnp.concatenate([uo, uo, uo], axis=0)
    for c in range(ncb):
        cols = slice(c * LANES, (c + 1) * LANES)
        dt = _softplus(dtraw[:, cols] + dtb_ref[...])
        a = dt * a_neg
        at = jnp.dot(jnp.concatenate(_split3(a), axis=1), uo3, preferred_element_type=f32)
        acum = at[:, :M_CHUNK]
        tot = at[:, M_CHUNK:]
        acum_s[c] = acum
        dt_s[c] = dt
        wd_s[c] = jnp.exp(tot - acum) * dt
        etot_s[c] = jnp.exp(tot)
        eac_s[c] = jnp.exp(acum)

    si = lax.broadcasted_iota(jnp.int32, (M_CHUNK, M_CHUNK), 0)
    ti = lax.broadcasted_iota(jnp.int32, (M_CHUNK, M_CHUNK), 1)
    causal = si <= ti

    def head_rows(g, hh):
        r = g * M_GROUP_ROWS + hh * M_HEADDIM
        return slice(r, r + M_HEADDIM)

    def decay_stage(c, g, slot):
        cols = slice(c * LANES, (c + 1) * LANES)
        b_n = b_s[g, cols, :].astype(bf16)
        c_tb = c_s[g * M_D_STATE:(g + 1) * M_D_STATE, cols].astype(bf16)
        sc_t = jnp.dot(b_n, c_tb, preferred_element_type=f32)
        for hh in range(M_HEADS_PER_GROUP):
            hrow = g * M_HEADS_PER_GROUP + hh
            hsl = slice(hh * M_HEADDIM, (hh + 1) * M_HEADDIM)
            rowb = jnp.broadcast_to(acum_s[c, hrow:hrow + 1, :], (M_CHUNK, M_CHUNK))
            dec = jnp.exp(jnp.where(causal, rowb - rowb.T, -1e30))
            mt_s[slot, hh] = (sc_t * dec).astype(bf16)
            xs_h = xc_s[head_rows(g, hh), cols]
            xdt_s[slot, hsl, :] = (xs_h * dt_s[c, hrow:hrow + 1, :]).astype(bf16)
            xw_s[slot, hsl, :] = (xs_h * wd_s[c, hrow:hrow + 1, :]).astype(bf16)

    def matmul_stage(c, g, slot):
        cols = slice(c * LANES, (c + 1) * LANES)
        grows = slice(g * M_GROUP_ROWS, (g + 1) * M_GROUP_ROWS)
        b_n = b_s[g, cols, :].astype(bf16)
        c_tb = c_s[g * M_D_STATE:(g + 1) * M_D_STATE, cols].astype(bf16)
        st_g = st_s[grows, :]
        yoff = jnp.dot(st_g.astype(bf16), c_tb, preferred_element_type=f32)
        for hh in range(M_HEADS_PER_GROUP):
            hrow = g * M_HEADS_PER_GROUP + hh
            hsl = slice(hh * M_HEADDIM, (hh + 1) * M_HEADDIM)
            rows = head_rows(g, hh)
            yd = jnp.dot(xdt_s[slot, hsl, :], mt_s[slot, hh], preferred_element_type=f32)
            y_s[rows, cols] = (yd + yoff[hsl] * eac_s[c, hrow:hrow + 1, :]
                               + dsk_ref[rows, :] * xc_s[rows, cols])
            st_s[rows, :] = st_g[hsl] * etot_s[c, hrow:hrow + 1, :]
        st_s[grows, :] += jnp.dot(xw_s[slot], b_n, preferred_element_type=f32)

    steps = [(c, g) for c in range(ncb) for g in range(M_GROUPS)]
    decay_stage(*steps[0], 0)
    for i, (c, g) in enumerate(steps):
        if i + 1 < len(steps):
            decay_stage(*steps[i + 1], (i + 1) % 2)
        matmul_stage(c, g, i % 2)

    for g in range(M_GROUPS):
        rows = slice(g * M_GROUP_ROWS, (g + 1) * M_GROUP_ROWS)
        zg = z_s[rows, :]
        yv = y_s[rows, :] * (zg * _sigmoid(zg))
        r = lax.rsqrt(jnp.mean(yv * yv, axis=0, keepdims=True) + EPS)
        for cb in range(ncb):
            sl = slice(cb * LANES, (cb + 1) * LANES)
            yn_s[rows, sl] = ((yv[:, sl] * r[:, sl]) * ng_ref[rows, :]).astype(bf16)
    out = jnp.dot(wo_ref[...], yn_s[...], preferred_element_type=f32)
    for cb in range(ncb):
        sl = slice(cb * LANES, (cb + 1) * LANES)
        o_ref[0, :, sl] = x[:, sl] + rep_s[2] * out[:, sl]


def _mod_spec(m):
    return pl.BlockSpec((1, 1, m.shape[2]), lambda b, j: (b, 0, 0))


def _mamba_layer(x, m, wz_t, wx, wdt_t, cw, cb, dtb, alog, dsk, ng, wo_t, *, x_token_major):
    if x_token_major:
        bsz, s, d = x.shape
    else:
        bsz, d, s = x.shape
    ts = min(MAMBA_TILE, s)
    ncb = ts // LANES
    x_spec = pl.BlockSpec((1, d, ts), lambda b, j: (b, 0, j))
    x_in_spec = pl.BlockSpec((1, ts, d), lambda b, j: (b, j, 0)) if x_token_major else x_spec
    small = pltpu.VMEM((ncb, M_HEADS, LANES), f32)
    return pl.pallas_call(
        functools.partial(_mamba_kernel, ts=ts, x_token_major=x_token_major),
        grid=(bsz, s // ts),
        in_specs=[x_in_spec, _mod_spec(m),
                  _const_spec(wz_t.shape), _const_spec(wx.shape), _const_spec(wdt_t.shape),
                  _const_spec(cw.shape), _const_spec(cb.shape), _const_spec(dtb.shape),
                  _const_spec(alog.shape), _const_spec(dsk.shape), _const_spec(ng.shape),
                  _const_spec(wo_t.shape)],
        out_specs=x_spec,
        out_shape=jax.ShapeDtypeStruct((bsz, d, s), f32),
        scratch_shapes=[
            pltpu.VMEM((3, d, LANES), f32),
            pltpu.VMEM((d, ts), bf16),
            pltpu.VMEM((CONV_HALO + ts, M_CONV_DIM), f32),
            pltpu.VMEM((M_D_INNER, ts), f32),
            pltpu.VMEM((M_GROUPS, ts, M_D_STATE), f32),
            pltpu.VMEM((M_GN, ts), f32),
            pltpu.VMEM((M_D_INNER, ts), f32),
            pltpu.VMEM((M_D_INNER, ts), f32),
            pltpu.VMEM((M_D_INNER, ts), bf16),
            pltpu.VMEM((2, M_HEADS_PER_GROUP, M_CHUNK, M_CHUNK), bf16),
            pltpu.VMEM((2, M_GROUP_ROWS, M_CHUNK), bf16),
            pltpu.VMEM((2, M_GROUP_ROWS, M_CHUNK), bf16),
            pltpu.VMEM((M_D_INNER, M_D_STATE), f32),
            small, small, small, small, small,
        ],
        compiler_params=pltpu.CompilerParams(
            dimension_semantics=("arbitrary", "arbitrary"), vmem_limit_bytes=VMEM_LIMIT),
        name="mamba_layer",
    )(x, m, wz_t, wx, wdt_t, cw, cb, dtb, alog, dsk, ng, wo_t)


def _kv_kernel(x_ref, m_ref, w_ref, k_ref, vt_ref, rep_s, h_s, *, ts):
    @pl.when(pl.program_id(1) == 0)
    def _():
        _stage_mods(m_ref, rep_s)

    _modulated_norm(x_ref[0], rep_s, h_s)
    kv = jnp.dot(w_ref[...], h_s[...], preferred_element_type=f32)
    k_ref[0] = kv[:SB_WIDTH].T.astype(bf16)
    for cb in range(ts // LANES):
        vt_ref[0, cb] = kv[SB_WIDTH:, cb * LANES:(cb + 1) * LANES].astype(bf16)


def _kv_proj(xt, m, wkv_t):
    bsz, d, s = xt.shape
    ts = min(KV_TILE, s)
    return pl.pallas_call(
        functools.partial(_kv_kernel, ts=ts),
        grid=(bsz, s // ts),
        in_specs=[pl.BlockSpec((1, d, ts), lambda b, j: (b, 0, j)), _mod_spec(m),
                  _const_spec(wkv_t.shape)],
        out_specs=[pl.BlockSpec((1, ts, SB_WIDTH), lambda b, j: (b, j, 0)),
                   pl.BlockSpec((1, ts // LANES, SB_WIDTH, LANES), lambda b, j: (b, j, 0, 0))],
        out_shape=[jax.ShapeDtypeStruct((bsz, s, SB_WIDTH), bf16),
                   jax.ShapeDtypeStruct((bsz, s // LANES, SB_WIDTH, LANES), bf16)],
        scratch_shapes=[pltpu.VMEM((2, d, LANES), f32), pltpu.VMEM((d, ts), bf16)],
        compiler_params=pltpu.CompilerParams(
            dimension_semantics=("arbitrary", "arbitrary"), vmem_limit_bytes=VMEM_LIMIT),
        name="kv_proj",
    )(xt, m, wkv_t)


def _sb_kernel(x_ref, m_ref, win_ref, wout_ref, k_ref, vt_ref,
               o_ref,
               rep_s, h_s, q_s, g_s, qp_s, acc_s, og_s, z_s, hl_s, rin_s, a_s,
               *, tq, final):
    j = pl.program_id(1)
    nsub = tq // SB_BLOCK

    @pl.when(j == 0)
    def _():
        _stage_mods(m_ref, rep_s)

    x = x_ref[0]
    _modulated_norm(x, rep_s, h_s)
    qz = jnp.dot(win_ref[...], h_s[...], preferred_element_type=f32)
    q_s[...] = (qz[:SB_WIDTH] * (SB_HEAD_DIM ** -0.5)).astype(bf16)
    gz = qz[SB_WIDTH:]
    g_s[...] = gz * _sigmoid(gz)

    row = lax.broadcasted_iota(jnp.int32, (SB_BLOCK, 2 * SB_BLOCK), 0)
    lane = lax.broadcasted_iota(jnp.int32, (SB_BLOCK, 2 * SB_BLOCK), 1)
    own_head = (row < SB_HEAD_DIM) == (lane < SB_BLOCK)
    strictly_before = row < (lane & (SB_BLOCK - 1))
    uu = jnp.where((lane & (SB_BLOCK - 1)) >= row, 1.0, 0.0).astype(bf16)
    top = lax.broadcasted_iota(jnp.int32, (SB_BLOCK, SB_BLOCK), 0) < SB_HEAD_DIM
    prows = [slice(p * SB_BLOCK, (p + 1) * SB_BLOCK) for p in range(SB_PAIRS)]

    def logits_stage(kb, buf, masked):
        k0 = pl.multiple_of(kb * SB_BLOCK, SB_BLOCK)
        for p in range(SB_PAIRS):
            kblk = k_ref[0, pl.ds(k0, SB_BLOCK), prows[p]]
            z_s[buf, p] = jnp.dot(kblk, qp_s[p], preferred_element_type=f32)
        for p in range(SB_PAIRS):
            z = z_s[buf, p]
            sp = _softplus_direct(z)
            if masked:
                sp = jnp.where(strictly_before, sp, 0.0)
                z_s[buf, p] = jnp.where(strictly_before, z, -1e30)
            hi = sp.astype(bf16)
            hl_s[buf, p, :SB_BLOCK] = hi
            hl_s[buf, p, SB_BLOCK:] = (sp - hi.astype(f32)).astype(bf16)

    def weights_stage(kb, buf, cs):
        for p in range(SB_PAIRS):
            rin_s[p] = jnp.dot(uu, hl_s[buf, p], preferred_element_type=f32)
        new_cs = []
        for p in range(SB_PAIRS):
            rin = rin_s[p]
            csr = cs[p:p + 1, :]
            a_s[p] = jnp.exp(z_s[buf, p] - rin - csr).astype(bf16)
            new_cs.append(csr + rin[0:1, :])
        for p in range(SB_PAIRS):
            acc_s[p] += jnp.dot(vt_ref[0, kb, prows[p], :], a_s[p], preferred_element_type=f32)
        return jnp.concatenate(new_cs, axis=0)

    for sub in range(nsub):
        qcols = slice(sub * SB_BLOCK, (sub + 1) * SB_BLOCK)
        qi = j * nsub + sub
        for p in range(SB_PAIRS):
            qp = q_s[prows[p], qcols]
            qp_s[p] = jnp.where(own_head, jnp.concatenate([qp, qp], axis=1), jnp.zeros_like(uu))
        acc_s[...] = jnp.zeros_like(acc_s)
        logits_stage(qi, 0, True)

        def cond(carry):
            kb, cs, _ = carry
            return jnp.logical_and(kb >= 0, jnp.min(cs) <= F32_EXP_FLUSH)

        def body(carry):
            kb, cs, parity = carry
            nxt = jnp.maximum(kb - 1, 0)

            def even(cs):
                logits_stage(nxt, 1, False)
                return weights_stage(kb, 0, cs)

            def odd(cs):
                logits_stage(nxt, 0, False)
                return weights_stage(kb, 1, cs)

            return kb - 1, lax.cond(parity == 0, even, odd, cs), 1 - parity

        lax.while_loop(cond, body, (qi, jnp.zeros((SB_PAIRS, 2 * SB_BLOCK), f32), jnp.int32(0)))

        for p in range(SB_PAIRS):
            acc = acc_s[p]
            o_pair = jnp.where(top, acc[:, :SB_BLOCK], acc[:, SB_BLOCK:])
            og_s[prows[p], qcols] = (o_pair * g_s[prows[p], qcols]).astype(bf16)

    out = jnp.dot(wout_ref[...], og_s[...], preferred_element_type=f32)
    xo = jnp.concatenate(
        [x[:, cb * LANES:(cb + 1) * LANES] + rep_s[2] * out[:, cb * LANES:(cb + 1) * LANES]
         for cb in range(tq // LANES)], axis=1)
    if final:
        r = lax.rsqrt(jnp.mean(xo * xo, axis=0, keepdims=True) + EPS)
        for cb in range(tq // LANES):
            sl = slice(cb * LANES, (cb + 1) * LANES)
            o_ref[0, sl, :] = ((xo[:, sl] * r[:, sl]) * rep_s[3]).T
    else:
        o_ref[0] = xo


def _sb_layer(xt, m, win_t, wout_t, k, vt, *, final):
    bsz, d, s = xt.shape
    tq = min(SB_TILE, s)
    x_spec = pl.BlockSpec((1, d, tq), lambda b, j: (b, 0, j))
    n_mods = m.shape[2] // d
    return pl.pallas_call(
        functools.partial(_sb_kernel, tq=tq, final=final),
        grid=(bsz, s // tq),
        in_specs=[x_spec, _mod_spec(m),
                  _const_spec(win_t.shape), _const_spec(wout_t.shape),
                  pl.BlockSpec((1, s, SB_WIDTH), lambda b, j: (b, 0, 0)),
                  pl.BlockSpec((1, s // LANES, SB_WIDTH, LANES), lambda b, j: (b, 0, 0, 0))],
        out_specs=pl.BlockSpec((1, tq, d), lambda b, j: (b, j, 0)) if final else x_spec,
        out_shape=jax.ShapeDtypeStruct((bsz, s, d) if final else (bsz, d, s), f32),
        scratch_shapes=[
            pltpu.VMEM((n_mods, d, LANES), f32),
            pltpu.VMEM((d, tq), bf16),
            pltpu.VMEM((SB_WIDTH, tq), bf16),
            pltpu.VMEM((SB_WIDTH, tq), f32),
            pltpu.VMEM((SB_PAIRS, SB_BLOCK, 2 * SB_BLOCK), bf16),
            pltpu.VMEM((SB_PAIRS, SB_BLOCK, 2 * SB_BLOCK), f32),
            pltpu.VMEM((SB_WIDTH, tq), bf16),
            pltpu.VMEM((2, SB_PAIRS, SB_BLOCK, 2 * SB_BLOCK), f32),
            pltpu.VMEM((2, SB_PAIRS, 2 * SB_BLOCK, 2 * SB_BLOCK), bf16),
            pltpu.VMEM((SB_PAIRS, SB_BLOCK, 2 * SB_BLOCK), f32),
            pltpu.VMEM((SB_PAIRS, SB_BLOCK, 2 * SB_BLOCK), bf16),
        ],
        compiler_params=pltpu.CompilerParams(
            dimension_semantics=("arbitrary", "arbitrary"), vmem_limit_bytes=VMEM_LIMIT),
        name="sb_layer",
    )(xt, m, win_t, wout_t, k, vt)


def _rep(v):
    return jnp.broadcast_to(v[..., None], v.shape + (LANES,))


def _layer_mods(mod, g, *extra):
    shift, scale, gate = jnp.split(mod, 3, axis=-1)
    parts = [g[None, :] * (1.0 + scale), shift, gate]
    parts += [jnp.broadcast_to(e[None, :], shift.shape) for e in extra]
    return jnp.concatenate(parts, axis=-1)[:, None, :]


def kernel(x, c, ada_w, ada_b, norm_g, m_in_w, m_conv_w, m_conv_b, m_dt_bias, m_a_log, m_d, m_norm_g, m_out_w, kv_ada_w, kv_ada_b, kv_norm_g, kv_w, sb_in_w, sb_out_w, final_g):
    mods = _ada(c, ada_w, ada_b)
    kv_mod = _ada(c, kv_ada_w[None], kv_ada_b[None])[0]

    xt = x
    for i in range(N_A_LAYERS):
        w = m_in_w[i]
        wz_t = w[:, :M_D_INNER].T.astype(bf16)
        wx = w[:, M_D_INNER:M_D_INNER + M_CONV_DIM].astype(bf16)
        wdt_t = w[:, M_D_INNER + M_CONV_DIM:].T.astype(bf16)
        xt = _mamba_layer(
            xt, _layer_mods(mods[i], norm_g[i]), wz_t, wx, wdt_t,
            m_conv_w[i], m_conv_b[i][None, :], _rep(m_dt_bias[i]), _rep(m_a_log[i]),
            _rep(jnp.repeat(m_d[i], M_HEADDIM)), _rep(m_norm_g[i]),
            m_out_w[i].T.astype(bf16), x_token_major=(i == 0))

    kv_shift, kv_scale = jnp.split(kv_mod, 2, axis=-1)
    kv_m = jnp.concatenate([kv_norm_g[None, :] * (1.0 + kv_scale), kv_shift], axis=-1)[:, None, :]
    k, vt = _kv_proj(xt, kv_m, kv_w.T.astype(bf16))

    for jb in range(N_B_LAYERS):
        i = N_A_LAYERS + jb
        final = jb == N_B_LAYERS - 1
        m = _layer_mods(mods[i], norm_g[i], *([final_g] if final else []))
        xt = _sb_layer(xt, m, sb_in_w[jb].T.astype(bf16), sb_out_w[jb].T.astype(bf16), k, vt,
                       final=final)
    return xt
```

```python
import functools

import jax
import jax.numpy as jnp
from jax import lax
from jax.experimental import pallas as pl
from jax.experimental.pallas import tpu as pltpu

D_MODEL = 1024
DEPTH = 4
N_A_LAYERS = DEPTH // 2
N_B_LAYERS = DEPTH - N_A_LAYERS

M_D_INNER = 2048
M_HEADDIM = 64
M_HEADS = M_D_INNER // M_HEADDIM
M_GROUPS = 4
M_HEADS_PER_GROUP = M_HEADS // M_GROUPS
M_D_STATE = 128
M_D_CONV = 4
M_CHUNK = 128
M_GN = M_GROUPS * M_D_STATE
M_CONV_DIM = M_D_INNER + 2 * M_GN
M_GROUP_ROWS = M_D_INNER // M_GROUPS
CONV_HALO = 8

SB_HEAD_DIM = 64
SB_HEADS = D_MODEL // SB_HEAD_DIM
SB_WIDTH = SB_HEADS * SB_HEAD_DIM
SB_BLOCK = 128
SB_PAIRS = SB_HEADS // 2

EPS = 1e-6
LANES = 128
F32_EXP_FLUSH = 87.3366
SOFTPLUS_LINEAR = 40.0

MAMBA_TILE = 256
U_PART = 256
Z_PART = 512
SB_TILE = 256
KV_TILE = 512
VMEM_LIMIT = 56 * 1024 * 1024

f32 = jnp.float32
bf16 = jnp.bfloat16


def _sigmoid(v):
    return 1.0 / (1.0 + jnp.exp(-v))


def _softplus(v):
    return jnp.maximum(v, 0.0) + jnp.log(1.0 + jnp.exp(-jnp.abs(v)))


def _softplus_direct(v):
    return jnp.where(v > SOFTPLUS_LINEAR, v, jnp.log(1.0 + jnp.exp(v)))


def _split3(v):
    hi = v.astype(bf16)
    r1 = v - hi.astype(f32)
    mid = r1.astype(bf16)
    lo = (r1 - mid.astype(f32)).astype(bf16)
    return hi, mid, lo


def _const_spec(shape):
    nd = len(shape)
    return pl.BlockSpec(shape, lambda *_: (0,) * nd, pipeline_mode=pl.Buffered(1))


def _stage_mods(m_ref, rep_s):
    for i in range(rep_s.shape[0]):
        row = m_ref[0, :, i * D_MODEL:(i + 1) * D_MODEL]
        rep_s[i] = jnp.broadcast_to(row, (LANES, D_MODEL)).T


def _modulated_norm(x, rep_s, h_ref):
    ts = x.shape[1]
    r = lax.rsqrt(jnp.mean(x * x, axis=0, keepdims=True) + EPS)
    for cb in range(ts // LANES):
        sl = slice(cb * LANES, (cb + 1) * LANES)
        h_ref[:, sl] = ((x[:, sl] * r[:, sl]) * rep_s[0] + rep_s[1]).astype(bf16)


def _ada_kernel(c_ref, w_ref, b_ref, o_ref):
    c = c_ref[...]
    ca = c * _sigmoid(c)
    o_ref[0] = jnp.dot(ca, w_ref[0], preferred_element_type=f32,
                       precision=lax.Precision.HIGHEST) + b_ref[0]


def _ada(c, w, b):
    nl, d, n = w.shape
    bsz = c.shape[0]
    tn = 1024
    return pl.pallas_call(
        _ada_kernel,
        grid=(nl, n // tn),
        in_specs=[pl.BlockSpec((bsz, d), lambda i, k: (0, 0)),
                  pl.BlockSpec((1, d, tn), lambda i, k: (i, 0, k)),
                  pl.BlockSpec((1, 1, tn), lambda i, k: (i, 0, k))],
        out_specs=pl.BlockSpec((1, bsz, tn), lambda i, k: (i, 0, k)),
        out_shape=jax.ShapeDtypeStruct((nl, bsz, n), f32),
        name="ada_mod",
    )(c, w, b.reshape(nl, 1, n))


def _mamba_kernel(x_ref, xn_ref, m_ref, wz_ref, wx_ref, wdt_ref, cw_ref, cb_ref,
                  dtb_ref, alog_ref, dsk_ref, ng_ref, wo_ref,
                  o_ref,
                  rep_s, h_s, z_s, u_s, dtr_s, halo_s, xc_s, b_s, c_s, yn_s, mt_s, xdt_s, xw_s, st_s,
                  acum_s, dt_s, wd_s, etot_s, eac_s,
                  *, ts, x_token_major):
    j = pl.program_id(1)
    ncb = ts // LANES
    halo = CONV_HALO

    def x_tile(ref, t):
        if x_token_major:
            return ref[0, t * ts:(t + 1) * ts, :].T
        return ref[0, :, t * ts:(t + 1) * ts]

    def projection_parts(x_fn, buf):
        def head():
            _modulated_norm(x_fn(), rep_s, h_s.at[buf])
            dtr_s[buf] = jnp.dot(wdt_ref[...], h_s[buf], preferred_element_type=f32)

        def u_part(c):
            cs_ = slice(c * U_PART, (c + 1) * U_PART)
            u_s[buf, halo:, cs_] = lax.dot_general(h_s[buf], wx_ref[:, cs_], (((0,), (0,)), ((), ())),
                                                   preferred_element_type=f32)

        def z_part(r):
            rs = slice(r * Z_PART, (r + 1) * Z_PART)
            z_s[buf, rs, :] = jnp.dot(wz_ref[rs, :], h_s[buf], preferred_element_type=f32)

        parts = [head]
        parts += [functools.partial(u_part, c) for c in range(M_CONV_DIM // U_PART)]
        parts += [functools.partial(z_part, r) for r in range(M_D_INNER // Z_PART)]
        return parts

    def project(x_fn, buf):
        for part in projection_parts(x_fn, buf):
            part()

    @pl.when(j == 0)
    def _():
        _stage_mods(m_ref, rep_s)
        st_s[...] = jnp.zeros_like(st_s)
        halo_s[...] = jnp.zeros_like(halo_s)
        project(lambda: x_tile(x_ref, 0), 0)

    a_neg = -jnp.exp(alog_ref[...])
    ri = lax.broadcasted_iota(jnp.int32, (M_CHUNK, 2 * M_CHUNK), 0)
    ci = lax.broadcasted_iota(jnp.int32, (M_CHUNK, 2 * M_CHUNK), 1)
    uo = jnp.where((ri <= ci) | (ci >= M_CHUNK), 1.0, 0.0).astype(bf16)
    uo3 = jnp.concatenate([uo, uo, uo], axis=0)
    si = lax.broadcasted_iota(jnp.int32, (M_CHUNK, M_CHUNK), 0)
    ti = lax.broadcasted_iota(jnp.int32, (M_CHUNK, M_CHUNK), 1)
    causal = si <= ti

    def conv_stage(buf, fill):
        u_s[buf, :halo, :] = halo_s[...]
        for blk in range(M_CONV_DIM // LANES):
            if blk:
                fill(blk - 1)
            cs_ = slice(blk * LANES, (blk + 1) * LANES)
            acc = cb_ref[:, cs_]
            for k in range(M_D_CONV):
                off = halo - (M_D_CONV - 1) + k
                acc = acc + cw_ref[k:k + 1, cs_] * u_s[buf, off:off + ts, cs_]
            act = acc * _sigmoid(acc)
            if blk < M_D_INNER // LANES:
                xc_s[cs_, :] = act.T
            elif blk < (M_D_INNER + M_GN) // LANES:
                b_s[blk - M_D_INNER // LANES] = act
            else:
                g = blk - (M_D_INNER + M_GN) // LANES
                c_s[g * M_D_STATE:(g + 1) * M_D_STATE, :] = act.T
        halo_s[...] = u_s[buf, ts:ts + halo, :]

    def decay_tables(buf):
        for c in range(ncb):
            cols = slice(c * LANES, (c + 1) * LANES)
            dt = _softplus(dtr_s[buf, :, cols] + dtb_ref[...])
            a = dt * a_neg
            at = jnp.dot(jnp.concatenate(_split3(a), axis=1), uo3, preferred_element_type=f32)
            acum = at[:, :M_CHUNK]
            tot = at[:, M_CHUNK:]
            acum_s[c] = acum
            dt_s[c] = dt
            wd_s[c] = jnp.exp(tot - acum) * dt
            etot_s[c] = jnp.exp(tot)
            eac_s[c] = jnp.exp(acum)

    def head_rows(g, hh):
        r = g * M_GROUP_ROWS + hh * M_HEADDIM
        return slice(r, r + M_HEADDIM)

    def decay_stage(c, g, slot):
        cols = slice(c * LANES, (c + 1) * LANES)
        b_n = b_s[g, cols, :].astype(bf16)
        c_tb = c_s[g * M_D_STATE:(g + 1) * M_D_STATE, cols].astype(bf16)
        sc_t = jnp.dot(b_n, c_tb, preferred_element_type=f32)
        for hh in range(M_HEADS_PER_GROUP):
            hrow = g * M_HEADS_PER_GROUP + hh
            hsl = slice(hh * M_HEADDIM, (hh + 1) * M_HEADDIM)
            rowb = jnp.broadcast_to(acum_s[c, hrow:hrow + 1, :], (M_CHUNK, M_CHUNK))
            dec = jnp.exp(jnp.where(causal, rowb - rowb.T, -1e30))
            mt_s[slot, hh] = (sc_t * dec).astype(bf16)
            xs_h = xc_s[head_rows(g, hh), cols]
            xdt_s[slot, hsl, :] = (xs_h * dt_s[c, hrow:hrow + 1, :]).astype(bf16)
            xw_s[slot, hsl, :] = (xs_h * wd_s[c, hrow:hrow + 1, :]).astype(bf16)

    def matmul_stage(c, g, slot):
        cols = slice(c * LANES, (c + 1) * LANES)
        grows = slice(g * M_GROUP_ROWS, (g + 1) * M_GROUP_ROWS)
        b_n = b_s[g, cols, :].astype(bf16)
        c_tb = c_s[g * M_D_STATE:(g + 1) * M_D_STATE, cols].astype(bf16)
        st_g = st_s[grows, :]
        yoff = jnp.dot(st_g.astype(bf16), c_tb, preferred_element_type=f32)
        for hh in range(M_HEADS_PER_GROUP):
            hrow = g * M_HEADS_PER_GROUP + hh
            hsl = slice(hh * M_HEADDIM, (hh + 1) * M_HEADDIM)
            rows = head_rows(g, hh)
            yd = jnp.dot(xdt_s[slot, hsl, :], mt_s[slot, hh], preferred_element_type=f32)
            xc_s[rows, cols] = (yd + yoff[hsl] * eac_s[c, hrow:hrow + 1, :]
                                + dsk_ref[rows, :] * xc_s[rows, cols])
            st_s[rows, :] = st_g[hsl] * etot_s[c, hrow:hrow + 1, :]
        st_s[grows, :] += jnp.dot(xw_s[slot], b_n, preferred_element_type=f32)

    def consume(buf, t, parts):
        n_conv = M_CONV_DIM // LANES
        n_early = len(parts) - 2
        conv_slots = {(i * n_conv) // n_early: i for i in reversed(range(n_early))}

        def fill(blk):
            if blk in conv_slots:
                parts[conv_slots[blk]]()

        conv_stage(buf, fill)
        fill(n_conv - 1)
        decay_tables(buf)
        steps = [(c, g) for c in range(ncb) for g in range(M_GROUPS)]
        decay_stage(*steps[0], 0)
        for i, (c, g) in enumerate(steps):
            if i + 1 < len(steps):
                decay_stage(*steps[i + 1], (i + 1) % 2)
            if i in (1, 4):
                parts[n_early + (i > 1)]()
            matmul_stage(c, g, i % 2)

        for g in range(M_GROUPS):
            rows = slice(g * M_GROUP_ROWS, (g + 1) * M_GROUP_ROWS)
            zg = z_s[buf, rows, :]
            yv = xc_s[rows, :] * (zg * _sigmoid(zg))
            r = lax.rsqrt(jnp.mean(yv * yv, axis=0, keepdims=True) + EPS)
            for cb in range(ncb):
                sl = slice(cb * LANES, (cb + 1) * LANES)
                yn_s[rows, sl] = ((yv[:, sl] * r[:, sl]) * ng_ref[rows, :]).astype(bf16)
        out = jnp.dot(wo_ref[...], yn_s[...], preferred_element_type=f32)
        x = x_tile(x_ref, t)
        for cb in range(ncb):
            sl = slice(cb * LANES, (cb + 1) * LANES)
            osl = slice(t * ts + cb * LANES, t * ts + (cb + 1) * LANES)
            o_ref[0, :, osl] = x[:, sl] + rep_s[2] * out[:, sl]

    consume(0, 0, projection_parts(lambda: x_tile(x_ref, 1), 1))
    consume(1, 1, projection_parts(lambda: x_tile(xn_ref, 0), 0))


def _mod_spec(m):
    return pl.BlockSpec((1, 1, m.shape[2]), lambda b, j: (b, 0, 0))


def _mamba_layer(x, m, wz_t, wx, wdt_t, cw, cb, dtb, alog, dsk, ng, wo_t, *, x_token_major):
    if x_token_major:
        bsz, s, d = x.shape
    else:
        bsz, d, s = x.shape
    ts = MAMBA_TILE
    blk = 2 * ts
    nblk = s // blk
    ncb = ts // LANES
    x_spec = pl.BlockSpec((1, d, blk), lambda b, j: (b, 0, j))

    def next_tile(j):
        return jnp.minimum(2 * j + 2, 2 * nblk - 1)

    if x_token_major:
        x_in_spec = pl.BlockSpec((1, blk, d), lambda b, j: (b, j, 0))
        xn_spec = pl.BlockSpec((1, ts, d), lambda b, j: (b, next_tile(j), 0))
    else:
        x_in_spec = x_spec
        xn_spec = pl.BlockSpec((1, d, ts), lambda b, j: (b, 0, next_tile(j)))
    small = pltpu.VMEM((ncb, M_HEADS, LANES), f32)
    return pl.pallas_call(
        functools.partial(_mamba_kernel, ts=ts, x_token_major=x_token_major),
        grid=(bsz, nblk),
        in_specs=[x_in_spec, xn_spec, _mod_spec(m),
                  _const_spec(wz_t.shape), _const_spec(wx.shape), _const_spec(wdt_t.shape),
                  _const_spec(cw.shape), _const_spec(cb.shape), _const_spec(dtb.shape),
                  _const_spec(alog.shape), _const_spec(dsk.shape), _const_spec(ng.shape),
                  _const_spec(wo_t.shape)],
        out_specs=x_spec,
        out_shape=jax.ShapeDtypeStruct((bsz, d, s), f32),
        scratch_shapes=[
            pltpu.VMEM((3, d, LANES), f32),
            pltpu.VMEM((2, d, ts), bf16),
            pltpu.VMEM((2, M_D_INNER, ts), f32),
            pltpu.VMEM((2, CONV_HALO + ts, M_CONV_DIM), f32),
            pltpu.VMEM((2, M_HEADS, ts), f32),
            pltpu.VMEM((CONV_HALO, M_CONV_DIM), f32),
            pltpu.VMEM((M_D_INNER, ts), f32),
            pltpu.VMEM((M_GROUPS, ts, M_D_STATE), f32),
            pltpu.VMEM((M_GN, ts), f32),
            pltpu.VMEM((M_D_INNER, ts), bf16),
            pltpu.VMEM((2, M_HEADS_PER_GROUP, M_CHUNK, M_CHUNK), bf16),
            pltpu.VMEM((2, M_GROUP_ROWS, M_CHUNK), bf16),
            pltpu.VMEM((2, M_GROUP_ROWS, M_CHUNK), bf16),
            pltpu.VMEM((M_D_INNER, M_D_STATE), f32),
            small, small, small, small, small,
        ],
        compiler_params=pltpu.CompilerParams(
            dimension_semantics=("arbitrary", "arbitrary"), vmem_limit_bytes=VMEM_LIMIT),
        name="mamba_layer",
    )(x, x, m, wz_t, wx, wdt_t, cw, cb, dtb, alog, dsk, ng, wo_t)


def _kv_kernel(x_ref, m_ref, w_ref, k_ref, vt_ref, rep_s, h_s, *, ts):
    @pl.when(pl.program_id(1) == 0)
    def _():
        _stage_mods(m_ref, rep_s)

    _modulated_norm(x_ref[0], rep_s, h_s)
    kv = jnp.dot(w_ref[...], h_s[...], preferred_element_type=f32)
    k_ref[0] = kv[:SB_WIDTH].T.astype(bf16)
    for cb in range(ts // LANES):
        vt_ref[0, cb] = kv[SB_WIDTH:, cb * LANES:(cb + 1) * LANES].astype(bf16)


def _kv_proj(xt, m, wkv_t):
    bsz, d, s = xt.shape
    ts = min(KV_TILE, s)
    return pl.pallas_call(
        functools.partial(_kv_kernel, ts=ts),
        grid=(bsz, s // ts),
        in_specs=[pl.BlockSpec((1, d, ts), lambda b, j: (b, 0, j)), _mod_spec(m),
                  _const_spec(wkv_t.shape)],
        out_specs=[pl.BlockSpec((1, ts, SB_WIDTH), lambda b, j: (b, j, 0)),
                   pl.BlockSpec((1, ts // LANES, SB_WIDTH, LANES), lambda b, j: (b, j, 0, 0))],
        out_shape=[jax.ShapeDtypeStruct((bsz, s, SB_WIDTH), bf16),
                   jax.ShapeDtypeStruct((bsz, s // LANES, SB_WIDTH, LANES), bf16)],
        scratch_shapes=[pltpu.VMEM((2, d, LANES), f32), pltpu.VMEM((d, ts), bf16)],
        compiler_params=pltpu.CompilerParams(
            dimension_semantics=("arbitrary", "arbitrary"), vmem_limit_bytes=VMEM_LIMIT),
        name="kv_proj",
    )(xt, m, wkv_t)


def _sb_kernel(x_ref, m_ref, win_ref, wout_ref, k_ref, vt_ref,
               o_ref,
               rep_s, h_s, q_s, g_s, qp_s, acc_s, og_s, z_s, hl_s, rin_s, a_s,
               *, tq, final):
    j = pl.program_id(1)
    nsub = tq // SB_BLOCK

    @pl.when(j == 0)
    def _():
        _stage_mods(m_ref, rep_s)

    x = x_ref[0]
    _modulated_norm(x, rep_s, h_s)
    qz = jnp.dot(win_ref[...], h_s[...], preferred_element_type=f32)
    q_s[...] = (qz[:SB_WIDTH] * (SB_HEAD_DIM ** -0.5)).astype(bf16)
    gz = qz[SB_WIDTH:]
    g_s[...] = gz * _sigmoid(gz)

    row = lax.broadcasted_iota(jnp.int32, (SB_BLOCK, 2 * SB_BLOCK), 0)
    lane = lax.broadcasted_iota(jnp.int32, (SB_BLOCK, 2 * SB_BLOCK), 1)
    own_head = (row < SB_HEAD_DIM) == (lane < SB_BLOCK)
    strictly_before = row < (lane & (SB_BLOCK - 1))
    uu = jnp.where((lane & (SB_BLOCK - 1)) >= row, 1.0, 0.0).astype(bf16)
    top = lax.broadcasted_iota(jnp.int32, (SB_BLOCK, SB_BLOCK), 0) < SB_HEAD_DIM
    prows = [slice(p * SB_BLOCK, (p + 1) * SB_BLOCK) for p in range(SB_PAIRS)]

    def logits_stage(kb, masked):
        k0 = pl.multiple_of(kb * SB_BLOCK, SB_BLOCK)
        for p in range(SB_PAIRS):
            kblk = k_ref[0, pl.ds(k0, SB_BLOCK), prows[p]]
            z_s[p] = jnp.dot(kblk, qp_s[p], preferred_element_type=f32)
        for p in range(SB_PAIRS):
            z = z_s[p]
            sp = _softplus_direct(z)
            if masked:
                sp = jnp.where(strictly_before, sp, 0.0)
                z_s[p] = jnp.where(strictly_before, z, -1e30)
            hi = sp.astype(bf16)
            hl_s[p, :SB_BLOCK] = hi
            hl_s[p, SB_BLOCK:] = (sp - hi.astype(f32)).astype(bf16)

    def weights_stage(kb, cs):
        for p in range(SB_PAIRS):
            rin_s[p] = jnp.dot(uu, hl_s[p], preferred_element_type=f32)
        new_cs = []
        for p in range(SB_PAIRS):
            rin = rin_s[p]
            csr = cs[p:p + 1, :]
            a_s[p] = jnp.exp(z_s[p] - rin - csr).astype(bf16)
            new_cs.append(csr + rin[0:1, :])
        for p in range(SB_PAIRS):
            acc_s[p] += jnp.dot(vt_ref[0, kb, prows[p], :], a_s[p], preferred_element_type=f32)
        return jnp.concatenate(new_cs, axis=0)

    for sub in range(nsub):
        qcols = slice(sub * SB_BLOCK, (sub + 1) * SB_BLOCK)
        qi = j * nsub + sub
        for p in range(SB_PAIRS):
            qp = q_s[prows[p], qcols]
            qp_s[p] = jnp.where(own_head, jnp.concatenate([qp, qp], axis=1), jnp.zeros_like(uu))
        acc_s[...] = jnp.zeros_like(acc_s)
        logits_stage(qi, True)
        cs0 = weights_stage(qi, jnp.zeros((SB_PAIRS, 2 * SB_BLOCK), f32))

        def cond(carry):
            kb, cs = carry
            return jnp.logical_and(kb >= 0, jnp.min(cs) <= F32_EXP_FLUSH)

        def body(carry):
            kb, cs = carry
            logits_stage(kb, False)
            return kb - 1, weights_stage(kb, cs)

        lax.while_loop(cond, body, (qi - 1, cs0))

        for p in range(SB_PAIRS):
            acc = acc_s[p]
            o_pair = jnp.where(top, acc[:, :SB_BLOCK], acc[:, SB_BLOCK:])
            og_s[prows[p], qcols] = (o_pair * g_s[prows[p], qcols]).astype(bf16)

    out = jnp.dot(wout_ref[...], og_s[...], preferred_element_type=f32)
    xo = jnp.concatenate(
        [x[:, cb * LANES:(cb + 1) * LANES] + rep_s[2] * out[:, cb * LANES:(cb + 1) * LANES]
         for cb in range(tq // LANES)], axis=1)
    if final:
        r = lax.rsqrt(jnp.mean(xo * xo, axis=0, keepdims=True) + EPS)
        for cb in range(tq // LANES):
            sl = slice(cb * LANES, (cb + 1) * LANES)
            o_ref[0, sl, :] = ((xo[:, sl] * r[:, sl]) * rep_s[3]).T
    else:
        o_ref[0] = xo


def _sb_layer(xt, m, win_t, wout_t, k, vt, *, final):
    bsz, d, s = xt.shape
    tq = min(SB_TILE, s)
    x_spec = pl.BlockSpec((1, d, tq), lambda b, j: (b, 0, j))
    n_mods = m.shape[2] // d
    return pl.pallas_call(
        functools.partial(_sb_kernel, tq=tq, final=final),
        grid=(bsz, s // tq),
        in_specs=[x_spec, _mod_spec(m),
                  _const_spec(win_t.shape), _const_spec(wout_t.shape),
                  pl.BlockSpec((1, s, SB_WIDTH), lambda b, j: (b, 0, 0)),
                  pl.BlockSpec((1, s // LANES, SB_WIDTH, LANES), lambda b, j: (b, 0, 0, 0))],
        out_specs=pl.BlockSpec((1, tq, d), lambda b, j: (b, j, 0)) if final else x_spec,
        out_shape=jax.ShapeDtypeStruct((bsz, s, d) if final else (bsz, d, s), f32),
        scratch_shapes=[
            pltpu.VMEM((n_mods, d, LANES), f32),
            pltpu.VMEM((d, tq), bf16),
            pltpu.VMEM((SB_WIDTH, tq), bf16),
            pltpu.VMEM((SB_WIDTH, tq), f32),
            pltpu.VMEM((SB_PAIRS, SB_BLOCK, 2 * SB_BLOCK), bf16),
            pltpu.VMEM((SB_PAIRS, SB_BLOCK, 2 * SB_BLOCK), f32),
            pltpu.VMEM((SB_WIDTH, tq), bf16),
            pltpu.VMEM((SB_PAIRS, SB_BLOCK, 2 * SB_BLOCK), f32),
            pltpu.VMEM((SB_PAIRS, 2 * SB_BLOCK, 2 * SB_BLOCK), bf16),
            pltpu.VMEM((SB_PAIRS, SB_BLOCK, 2 * SB_BLOCK), f32),
            pltpu.VMEM((SB_PAIRS, SB_BLOCK, 2 * SB_BLOCK), bf16),
        ],
        compiler_params=pltpu.CompilerParams(
            dimension_semantics=("arbitrary", "arbitrary"), vmem_limit_bytes=VMEM_LIMIT),
        name="sb_layer",
    )(xt, m, win_t, wout_t, k, vt)


def _rep(v):
    return jnp.broadcast_to(v[..., None], v.shape + (LANES,))


def _layer_mods(mod, g, *extra):
    shift, scale, gate = jnp.split(mod, 3, axis=-1)
    parts = [g[None, :] * (1.0 + scale), shift, gate]
    parts += [jnp.broadcast_to(e[None, :], shift.shape) for e in extra]
    return jnp.concatenate(parts, axis=-1)[:, None, :]


def kernel(x, c, ada_w, ada_b, norm_g, m_in_w, m_conv_w, m_conv_b, m_dt_bias, m_a_log, m_d, m_norm_g, m_out_w, kv_ada_w, kv_ada_b, kv_norm_g, kv_w, sb_in_w, sb_out_w, final_g):
    mods = _ada(c, ada_w, ada_b)
    kv_mod = _ada(c, kv_ada_w[None], kv_ada_b[None])[0]

    xt = x
    for i in range(N_A_LAYERS):
        w = m_in_w[i]
        wz_t = w[:, :M_D_INNER].T.astype(bf16)
        wx = w[:, M_D_INNER:M_D_INNER + M_CONV_DIM].astype(bf16)
        wdt_t = w[:, M_D_INNER + M_CONV_DIM:].T.astype(bf16)
        xt = _mamba_layer(
            xt, _layer_mods(mods[i], norm_g[i]), wz_t, wx, wdt_t,
            m_conv_w[i], m_conv_b[i][None, :], _rep(m_dt_bias[i]), _rep(m_a_log[i]),
            _rep(jnp.repeat(m_d[i], M_HEADDIM)), _rep(m_norm_g[i]),
            m_out_w[i].T.astype(bf16), x_token_major=(i == 0))

    kv_shift, kv_scale = jnp.split(kv_mod, 2, axis=-1)
    kv_m = jnp.concatenate([kv_norm_g[None, :] * (1.0 + kv_scale), kv_shift], axis=-1)[:, None, :]
    k, vt = _kv_proj(xt, kv_m, kv_w.T.astype(bf16))

    for jb in range(N_B_LAYERS):
        i = N_A_LAYERS + jb
        final = jb == N_B_LAYERS - 1
        m = _layer_mods(mods[i], norm_g[i], *([final_g] if final else []))
        xt = _sb_layer(xt, m, sb_in_w[jb].T.astype(bf16), sb_out_w[jb].T.astype(bf16), k, vt,
                       final=final)
    return xt
```
